```python
import math
import jax, jax.numpy as jnp
from jax import lax
import numpy as np

D_MODEL = 2048
BATCH = 4
SEQ = 4096
DEPTH = 1

CHUNK = 64
Q_BLOCK = 128
EPS = 1e-6
NEG_INF = -1e30
ROPE_THETA = 10000.0

MLA_HEADS = 8
Q_LORA = 512
KV_LORA = 256
QK_NOPE = 128
QK_ROPE = 64
V_HEAD = 128
MLA_WIDTH = MLA_HEADS * V_HEAD

FOX_HEADS = 8
FOX_HEAD_DIM = 128
FOX_WIDTH = FOX_HEADS * FOX_HEAD_DIM

MIX_WIDTH = MLA_WIDTH + FOX_WIDTH
IN_SPLITS = (Q_LORA, KV_LORA, QK_ROPE, FOX_WIDTH, FOX_WIDTH, FOX_WIDTH, FOX_HEADS)
IN_DIM = sum(IN_SPLITS)

FFN_HIDDEN = ((8 * D_MODEL // 3 + 255) // 256) * 256

kernel_name = "hybrid_mla_fox_parallel_heads"


def rms_norm(x, g):
    xf = x.astype(jnp.float32)
    y = xf * lax.rsqrt(jnp.mean(xf * xf, axis=-1, keepdims=True) + EPS)
    return y.astype(x.dtype) * g


def apply_rope(x, cos, sin):
    half = x.shape[-1] // 2
    x1, x2 = x[..., :half], x[..., half:]
    return jnp.concatenate([x1 * cos - x2 * sin, x2 * cos + x1 * sin], axis=-1)


def block_attention(q, k, v, scale, causal_unit, log_decay=None):
    B, H, S, _ = q.shape
    dv = v.shape[-1]
    n_blocks = S // Q_BLOCK
    key_unit = jnp.arange(S) // causal_unit

    def one_block(i):
        start = i * Q_BLOCK
        q_i = lax.dynamic_slice_in_dim(q, start, Q_BLOCK, axis=2)
        t = start + jnp.arange(Q_BLOCK)
        s = jnp.einsum('bhqd,bhkd->bhqk', q_i, k).astype(jnp.float32) * scale
        if log_decay is not None:
            c_i = lax.dynamic_slice_in_dim(log_decay, start, Q_BLOCK, axis=2)
            s = s + c_i[..., :, None] - log_decay[..., None, :]
        allowed = key_unit[None, :] <= (t // causal_unit)[:, None]
        s = jnp.where(allowed, s, NEG_INF)
        p = jax.nn.softmax(s, axis=-1).astype(v.dtype)
        return jnp.einsum('bhqk,bhkd->bhqd', p, v)

    out = lax.map(one_block, jnp.arange(n_blocks))
    return out.transpose(1, 2, 0, 3, 4).reshape(B, H, S, dv)


def setup_inputs(seed: int = 0) -> dict:
    key = jax.random.key(seed)
    ks = jax.random.split(key, 20)

    def nrm(k, shape, fan_in):
        return jax.random.normal(k, shape, jnp.float32) * fan_in ** -0.5

    def gain(k, shape):
        return 1.0 + 0.02 * jax.random.normal(k, shape, jnp.float32)

    x = jax.random.normal(ks[0], (BATCH, SEQ, D_MODEL), jnp.float32)
    offset = jax.random.randint(ks[1], (BATCH,), 0, 4096, dtype=jnp.int32)
    positions = (jnp.arange(SEQ, dtype=jnp.int32)[None, :] + offset[:, None]).astype(jnp.int32)
    return {
        "x": x,
        "positions": positions,
        "g_attn_norm": gain(ks[2], (DEPTH, D_MODEL)),
        "w_in": nrm(ks[3], (DEPTH, D_MODEL, IN_DIM), D_MODEL),
        "b_forget": jax.random.uniform(ks[4], (DEPTH, FOX_HEADS), jnp.float32, 1.0, 4.0),
        "g_q_lat": gain(ks[5], (DEPTH, Q_LORA)),
        "w_uq": nrm(ks[6], (DEPTH, Q_LORA, MLA_HEADS * (QK_NOPE + QK_ROPE)), Q_LORA),
        "g_kv_lat": gain(ks[7], (DEPTH, KV_LORA)),
        "w_ukv": nrm(ks[8], (DEPTH, KV_LORA, MLA_HEADS * (QK_NOPE + V_HEAD)), KV_LORA),
        "g_out_mla": gain(ks[9], (DEPTH, MLA_WIDTH)),
        "g_out_fox": gain(ks[10], (DEPTH, FOX_WIDTH)),
        "w_out": nrm(ks[11], (DEPTH, MIX_WIDTH, D_MODEL), MIX_WIDTH),
        "g_ffn_norm": gain(ks[12], (DEPTH, D_MODEL)),
        "w_gate": nrm(ks[13], (DEPTH, D_MODEL, FFN_HIDDEN), D_MODEL),
        "w_up": nrm(ks[14], (DEPTH, D_MODEL, FFN_HIDDEN), D_MODEL),
        "w_down": nrm(ks[15], (DEPTH, FFN_HIDDEN, D_MODEL), FFN_HIDDEN),
        "g_final_norm": gain(ks[16], (D_MODEL,)),
    }


def reference(x, positions, g_attn_norm, w_in, b_forget, g_q_lat, w_uq, g_kv_lat,
              w_ukv, g_out_mla, g_out_fox, w_out, g_ffn_norm, w_gate, w_up, w_down,
              g_final_norm):
    B, S, _ = x.shape
    inv_freq = ROPE_THETA ** (-jnp.arange(0, QK_ROPE, 2, dtype=jnp.float32) / QK_ROPE)
    ang = positions.astype(jnp.float32)[..., None] * inv_freq
    cos = jnp.cos(ang)[:, :, None, :].astype(x.dtype)
    sin = jnp.sin(ang)[:, :, None, :].astype(x.dtype)
    split_idx = list(np.cumsum(IN_SPLITS)[:-1])

    for l in range(DEPTH):
        h = rms_norm(x, g_attn_norm[l])
        proj = h @ w_in[l]
        q_lat, kv_lat, k_rope, fq, fk, fv, f_logit = jnp.split(proj, split_idx, axis=-1)

        q = (rms_norm(q_lat, g_q_lat[l]) @ w_uq[l]).reshape(B, S, MLA_HEADS, QK_NOPE + QK_ROPE)
        q_nope, q_pe = q[..., :QK_NOPE], q[..., QK_NOPE:]
        q_pe = apply_rope(q_pe, cos, sin)
        kv = (rms_norm(kv_lat, g_kv_lat[l]) @ w_ukv[l]).reshape(B, S, MLA_HEADS, QK_NOPE + V_HEAD)
        k_nope, v_mla = kv[..., :QK_NOPE], kv[..., QK_NOPE:]
        k_pe = apply_rope(k_rope[:, :, None, :], cos, sin)
        k_pe = jnp.broadcast_to(k_pe, (B, S, MLA_HEADS, QK_ROPE))
        q_mla = jnp.concatenate([q_nope, q_pe], axis=-1).transpose(0, 2, 1, 3)
        k_mla = jnp.concatenate([k_nope, k_pe], axis=-1).transpose(0, 2, 1, 3)
        v_mla = v_mla.transpose(0, 2, 1, 3)
        o_mla = block_attention(q_mla, k_mla, v_mla, 1.0 / math.sqrt(QK_NOPE + QK_ROPE), CHUNK)
        o_mla = o_mla.transpose(0, 2, 1, 3).reshape(B, S, MLA_WIDTH)

        q_fox = fq.reshape(B, S, FOX_HEADS, FOX_HEAD_DIM).transpose(0, 2, 1, 3)
        k_fox = fk.reshape(B, S, FOX_HEADS, FOX_HEAD_DIM).transpose(0, 2, 1, 3)
        v_fox = fv.reshape(B, S, FOX_HEADS, FOX_HEAD_DIM).transpose(0, 2, 1, 3)
        log_f = jax.nn.log_sigmoid(f_logit.astype(jnp.float32) + b_forget[l].astype(jnp.float32))
        c = jnp.cumsum(log_f.transpose(0, 2, 1), axis=-1)
        o_fox = block_attention(q_fox, k_fox, v_fox, 1.0 / math.sqrt(FOX_HEAD_DIM), 1, c)
        o_fox = o_fox.transpose(0, 2, 1, 3).reshape(B, S, FOX_WIDTH)

        mixed = jnp.concatenate([rms_norm(o_mla, g_out_mla[l]), rms_norm(o_fox, g_out_fox[l])], axis=-1)
        x = x + mixed @ w_out[l]

        h2 = rms_norm(x, g_ffn_norm[l])
        x = x + (jax.nn.silu(h2 @ w_gate[l]) * (h2 @ w_up[l])) @ w_down[l]

    return rms_norm(x, g_final_norm)
```

```python
import functools
import math

import numpy as np
import jax
import jax.numpy as jnp
from jax import lax
from jax.experimental import pallas as pl
from jax.experimental.pallas import tpu as pltpu

D_MODEL = 2048
CHUNK = 64
EPS = 1e-6
ROPE_THETA = 10000.0
MLA_HEADS = 8
Q_LORA = 512
KV_LORA = 256
QK_NOPE = 128
QK_ROPE = 64
V_HEAD = 128
FOX_HEADS = 8
FOX_HEAD_DIM = 128
HEAD_WIDTH = 1024
FFN_HIDDEN = 5632

LANES = 128
QK_WIDTH = 256
LOG2E = math.log2(math.e)
MASK_VALUE = -1e30

IN_TM = 512
IN_TN = 1024
UP_TM = 512
ATT_T = 512
OUT_TM = 512
FFN_TM = 512
FFN_TH = 512

_BF16 = jnp.bfloat16
_F32 = jnp.float32


def _rms_scale(x):
    return x * lax.rsqrt(jnp.mean(x * x, axis=-1, keepdims=True) + EPS)


def _split3_bf16(x):
    hi = x.astype(_BF16)
    r = x - hi.astype(_F32)
    mid = r.astype(_BF16)
    lo = (r - mid.astype(_F32)).astype(_BF16)
    return hi, mid, lo


def _in_proj_kernel(x_ref, g_ref, w_ref, b_ref, misc_ref, q_ref, k_ref, v_ref,
                    h_scr, carry_scr, dec_scr, *, tiles_per_seq, q_scale):
    i = pl.program_id(0)
    j = pl.program_id(1)
    tm = x_ref.shape[0]

    @pl.when(j == 0)
    def _():
        h_scr[...] = (_rms_scale(x_ref[...]) * g_ref[...]).astype(_BF16)

    acc = jnp.dot(h_scr[...], w_ref[...], preferred_element_type=_F32)
    lane = lax.broadcasted_iota(jnp.int32, (tm, LANES), 1)

    @pl.when(j == 0)
    def _():
        misc_ref[...] = acc.astype(_BF16)
        f = acc[:, IN_TN - LANES:] + b_ref[...]
        log_f = jnp.minimum(f, 0.0) - jnp.log1p(jnp.exp(-jnp.abs(f)))
        r = lax.broadcasted_iota(jnp.int32, (tm, tm), 0)
        c = lax.broadcasted_iota(jnp.int32, (tm, tm), 1)
        tri = jnp.where(c <= r, 1.0, 0.0).astype(_BF16)
        hi, mid, lo = _split3_bf16(log_f)
        cum = (jnp.dot(tri, hi, preferred_element_type=_F32)
               + jnp.dot(tri, mid, preferred_element_type=_F32)
               + jnp.dot(tri, lo, preferred_element_type=_F32))

        @pl.when(i % tiles_per_seq == 0)
        def _():
            carry_scr[...] = jnp.zeros_like(carry_scr)

        cum = cum + carry_scr[...]
        carry_scr[...] = cum[tm - 1:tm, :]
        dhi, dmid, dlo = _split3_bf16(cum * (-LOG2E))
        zero = jnp.zeros_like(dhi)
        dec_scr[...] = jnp.where(lane < 8, dhi, jnp.where(lane < 16, dmid, jnp.where(lane < 24, dlo, zero)))

    @pl.when(j == 1)
    def _():
        for h in range(FOX_HEADS):
            q_ref[h, :, 0:LANES] = (acc[:, h * LANES:(h + 1) * LANES] * q_scale).astype(_BF16)
            pick = (lane == h) | (lane == 8 + h) | (lane == 16 + h)
            q_ref[h, :, LANES:QK_WIDTH] = jnp.where(pick, 1.0, 0.0).astype(_BF16)

    @pl.when(j == 2)
    def _():
        for h in range(FOX_HEADS):
            k_ref[h, :, 0:LANES] = acc[:, h * LANES:(h + 1) * LANES].astype(_BF16)
            k_ref[h, :, LANES:QK_WIDTH] = dec_scr[...]

    @pl.when(j == 3)
    def _():
        for h in range(FOX_HEADS):
            v_ref[h] = acc[:, h * LANES:(h + 1) * LANES].astype(_BF16)


def _in_proj(x2d, g, w_in_r, b_row, seq):
    T = x2d.shape[0]
    tm = IN_TM
    grid = (T // tm, w_in_r.shape[1] // IN_TN)
    kern = functools.partial(_in_proj_kernel, tiles_per_seq=seq // tm,
                             q_scale=LOG2E / math.sqrt(FOX_HEAD_DIM))
    return pl.pallas_call(
        kern,
        grid=grid,
        in_specs=[
            pl.BlockSpec((tm, D_MODEL), lambda i, j: (i, 0)),
            pl.BlockSpec((1, D_MODEL), lambda i, j: (0, 0)),
            pl.BlockSpec((D_MODEL, IN_TN), lambda i, j: (0, j)),
            pl.BlockSpec((1, LANES), lambda i, j: (0, 0)),
        ],
        out_specs=[
            pl.BlockSpec((tm, IN_TN), lambda i, j: (i, 0)),
            pl.BlockSpec((FOX_HEADS, tm, QK_WIDTH), lambda i, j: (0, i, 0)),
            pl.BlockSpec((FOX_HEADS, tm, QK_WIDTH), lambda i, j: (0, i, 0)),
            pl.BlockSpec((FOX_HEADS, tm, LANES), lambda i, j: (0, i, 0)),
        ],
        out_shape=[
            jax.ShapeDtypeStruct((T, IN_TN), _BF16),
            jax.ShapeDtypeStruct((FOX_HEADS, T, QK_WIDTH), _BF16),
            jax.ShapeDtypeStruct((FOX_HEADS, T, QK_WIDTH), _BF16),
            jax.ShapeDtypeStruct((FOX_HEADS, T, LANES), _BF16),
        ],
        scratch_shapes=[
            pltpu.VMEM((tm, D_MODEL), _BF16),
            pltpu.VMEM((1, LANES), _F32),
            pltpu.VMEM((tm, LANES), _BF16),
        ],
        compiler_params=pltpu.CompilerParams(
            dimension_semantics=("arbitrary", "arbitrary"),
            vmem_limit_bytes=48 * 1024 * 1024),
        name="in_proj",
    )(x2d, g, w_in_r, b_row)


def _mla_up_kernel(misc_ref, pos_ref, invf_ref, gq_ref, gkv_ref, wuq_ref, wukv_ref,
                   q_ref, k_ref, v_ref, *, q_scale):
    tm = misc_ref.shape[0]
    q_lat = misc_ref[:, 0:Q_LORA].astype(_F32)
    kv_lat = misc_ref[:, Q_LORA:Q_LORA + KV_LORA].astype(_F32)
    k_rope = misc_ref[:, Q_LORA + KV_LORA:Q_LORA + KV_LORA + LANES].astype(_F32)

    qn = (_rms_scale(q_lat) * gq_ref[...]).astype(_BF16)
    kvn = (_rms_scale(kv_lat) * gkv_ref[...]).astype(_BF16)
    q = jnp.dot(qn, wuq_ref[...], preferred_element_type=_F32)
    kv = jnp.dot(kvn, wukv_ref[...], preferred_element_type=_F32)

    ang = pos_ref[...] * invf_ref[...]
    cos = jnp.cos(ang)
    sin = jnp.sin(ang)
    lane = lax.broadcasted_iota(jnp.int32, (tm, LANES), 1)
    half = QK_ROPE // 2
    sin_signed = jnp.where((lane % QK_ROPE) < half, -sin, sin)

    def rope(p):
        return p * cos + pltpu.roll(p, half, 1) * sin_signed

    k_pe = jnp.where(lane < QK_ROPE, rope(k_rope), 0.0).astype(_BF16)
    for h in range(MLA_HEADS):
        base = h * QK_WIDTH
        q_ref[h, :, 0:LANES] = (q[:, base:base + LANES] * q_scale).astype(_BF16)
        q_ref[h, :, LANES:QK_WIDTH] = (rope(q[:, base + LANES:base + QK_WIDTH]) * q_scale).astype(_BF16)
        k_ref[h, :, 0:LANES] = kv[:, base:base + LANES].astype(_BF16)
        k_ref[h, :, LANES:QK_WIDTH] = k_pe
        v_ref[h] = kv[:, base + LANES:base + QK_WIDTH].astype(_BF16)


def _mla_up(misc, pos_f, invf_row, gq, gkv, w_uq_r, w_ukv):
    T = misc.shape[0]
    tm = UP_TM
    kern = functools.partial(_mla_up_kernel, q_scale=LOG2E / math.sqrt(QK_NOPE + QK_ROPE))
    return pl.pallas_call(
        kern,
        grid=(T // tm,),
        in_specs=[
            pl.BlockSpec((tm, IN_TN), lambda i: (i, 0)),
            pl.BlockSpec((tm, 1), lambda i: (i, 0)),
            pl.BlockSpec((1, LANES), lambda i: (0, 0)),
            pl.BlockSpec((1, Q_LORA), lambda i: (0, 0)),
            pl.BlockSpec((1, KV_LORA), lambda i: (0, 0)),
            pl.BlockSpec((Q_LORA, MLA_HEADS * QK_WIDTH), lambda i: (0, 0)),
            pl.BlockSpec((KV_LORA, MLA_HEADS * QK_WIDTH), lambda i: (0, 0)),
        ],
        out_specs=[
            pl.BlockSpec((MLA_HEADS, tm, QK_WIDTH), lambda i: (0, i, 0)),
            pl.BlockSpec((MLA_HEADS, tm, QK_WIDTH), lambda i: (0, i, 0)),
            pl.BlockSpec((MLA_HEADS, tm, LANES), lambda i: (0, i, 0)),
        ],
        out_shape=[
            jax.ShapeDtypeStruct((MLA_HEADS, T, QK_WIDTH), _BF16),
            jax.ShapeDtypeStruct((MLA_HEADS, T, QK_WIDTH), _BF16),
            jax.ShapeDtypeStruct((MLA_HEADS, T, LANES), _BF16),
        ],
        compiler_params=pltpu.CompilerParams(
            dimension_semantics=("arbitrary",),
            vmem_limit_bytes=40 * 1024 * 1024),
        name="mla_up",
    )(misc, pos_f, invf_row, gq, gkv, w_uq_r, w_ukv)


def _attn_kernel(q_ref, k_ref, v_ref, o_ref, *, causal_unit):
    seq = q_ref.shape[0]
    t = ATT_T
    nt = seq // t
    row = lax.broadcasted_iota(jnp.int32, (t, t), 0)
    col = lax.broadcasted_iota(jnp.int32, (t, t), 1)
    if causal_unit == 1:
        allowed = col <= row
    else:
        allowed = (col // causal_unit) <= (row // causal_unit)

    def scores(q, ks):
        k = k_ref[pl.ds(ks, t), :]
        return lax.dot_general(q, k, (((1,), (1,)), ((), ())), preferred_element_type=_F32)

    def update(s, ks, carry):
        m, l, acc = carry
        m_new = jnp.maximum(m, jnp.max(s, axis=1, keepdims=True))
        alpha = jnp.exp2(m - m_new)
        p = jnp.exp2(s - m_new)
        l = alpha * l + jnp.sum(p, axis=1, keepdims=True)
        v = v_ref[pl.ds(ks, t), :]
        acc = alpha * acc + jnp.dot(p.astype(_BF16), v, preferred_element_type=_F32)
        return m_new, l, acc

    def q_tile(i, _):
        qs = pl.multiple_of(i * t, t)
        q = q_ref[pl.ds(qs, t), :]

        def kv_step(j, carry):
            ks = pl.multiple_of(j * t, t)
            return update(scores(q, ks), ks, carry)

        init = (jnp.full((t, 1), MASK_VALUE, _F32), jnp.zeros((t, 1), _F32),
                jnp.zeros((t, LANES), _F32))
        carry = lax.fori_loop(0, i, kv_step, init)
        s = jnp.where(allowed, scores(q, qs), MASK_VALUE)
        m, l, acc = update(s, qs, carry)
        o_ref[pl.ds(qs, t), :] = (acc / l).astype(o_ref.dtype)
        return 0

    lax.fori_loop(0, nt, q_tile, 0)


def _attention(q, k, v, batch, seq, causal_unit, name):
    heads = q.shape[0]
    kern = functools.partial(_attn_kernel, causal_unit=causal_unit)
    return pl.pallas_call(
        kern,
        grid=(heads, batch),
        in_specs=[
            pl.BlockSpec((None, seq, QK_WIDTH), lambda h, b: (h, b, 0)),
            pl.BlockSpec((None, seq, QK_WIDTH), lambda h, b: (h, b, 0)),
            pl.BlockSpec((None, seq, LANES), lambda h, b: (h, b, 0)),
        ],
        out_specs=pl.BlockSpec((seq, LANES), lambda h, b: (b, h)),
        out_shape=jax.ShapeDtypeStruct((batch * seq, heads * LANES), _BF16),
        compiler_params=pltpu.CompilerParams(
            dimension_semantics=("arbitrary", "arbitrary"),
            vmem_limit_bytes=40 * 1024 * 1024),
        name=name,
    )(q, k, v)


def _out_proj_kernel(om_ref, of_ref, gm_ref, gf_ref, w_ref, x_ref, o_ref):
    om = (_rms_scale(om_ref[...].astype(_F32)) * gm_ref[...]).astype(_BF16)
    of = (_rms_scale(of_ref[...].astype(_F32)) * gf_ref[...]).astype(_BF16)
    y = jnp.dot(om, w_ref[0:HEAD_WIDTH, :], preferred_element_type=_F32)
    y = y + jnp.dot(of, w_ref[HEAD_WIDTH:2 * HEAD_WIDTH, :], preferred_element_type=_F32)
    o_ref[...] = x_ref[...] + y


def _out_proj(o_mla, o_fox, gm, gf, w_out, x2d):
    T = x2d.shape[0]
    tm = OUT_TM
    return pl.pallas_call(
        _out_proj_kernel,
        grid=(T // tm,),
        in_specs=[
            pl.BlockSpec((tm, HEAD_WIDTH), lambda i: (i, 0)),
            pl.BlockSpec((tm, HEAD_WIDTH), lambda i: (i, 0)),
            pl.BlockSpec((1, HEAD_WIDTH), lambda i: (0, 0)),
            pl.BlockSpec((1, HEAD_WIDTH), lambda i: (0, 0)),
            pl.BlockSpec((2 * HEAD_WIDTH, D_MODEL), lambda i: (0, 0)),
            pl.BlockSpec((tm, D_MODEL), lambda i: (i, 0)),
        ],
        out_specs=pl.BlockSpec((tm, D_MODEL), lambda i: (i, 0)),
        out_shape=jax.ShapeDtypeStruct((T, D_MODEL), _F32),
        compiler_params=pltpu.CompilerParams(
            dimension_semantics=("arbitrary",),
            vmem_limit_bytes=48 * 1024 * 1024),
        name="out_proj",
    )(o_mla, o_fox, gm, gf, w_out, x2d)


def _ffn_kernel(x_ref, g_ref, wg_ref, wu_ref, wd_ref, gfin_ref, o_ref, h_scr, acc_scr):
    j = pl.program_id(1)

    @pl.when(j == 0)
    def _():
        h_scr[...] = (_rms_scale(x_ref[...]) * g_ref[...]).astype(_BF16)
        acc_scr[...] = jnp.zeros_like(acc_scr)

    h = h_scr[...]
    gate = jnp.dot(h, wg_ref[...], preferred_element_type=_F32)
    up = jnp.dot(h, wu_ref[...], preferred_element_type=_F32)
    a = (gate * (1.0 / (1.0 + jnp.exp(-gate))) * up).astype(_BF16)
    acc_scr[...] += jnp.dot(a, wd_ref[...], preferred_element_type=_F32)

    @pl.when(j == pl.num_programs(1) - 1)
    def _():
        y = x_ref[...] + acc_scr[...]
        o_ref[...] = _rms_scale(y) * gfin_ref[...]


def _ffn(x1, g, w_gate, w_up, w_down, g_final):
    T = x1.shape[0]
    tm, th = FFN_TM, FFN_TH
    return pl.pallas_call(
        _ffn_kernel,
        grid=(T // tm, FFN_HIDDEN // th),
        in_specs=[
            pl.BlockSpec((tm, D_MODEL), lambda i, j: (i, 0)),
            pl.BlockSpec((1, D_MODEL), lambda i, j: (0, 0)),
            pl.BlockSpec((D_MODEL, th), lambda i, j: (0, j)),
            pl.BlockSpec((D_MODEL, th), lambda i, j: (0, j)),
            pl.BlockSpec((th, D_MODEL), lambda i, j: (j, 0)),
            pl.BlockSpec((1, D_MODEL), lambda i, j: (0, 0)),
        ],
        out_specs=pl.BlockSpec((tm, D_MODEL), lambda i, j: (i, 0)),
        out_shape=jax.ShapeDtypeStruct((T, D_MODEL), _F32),
        scratch_shapes=[
            pltpu.VMEM((tm, D_MODEL), _BF16),
            pltpu.VMEM((tm, D_MODEL), _F32),
        ],
        compiler_params=pltpu.CompilerParams(
            dimension_semantics=("arbitrary", "arbitrary"),
            vmem_limit_bytes=48 * 1024 * 1024),
        name="ffn",
    )(x1, g, w_gate, w_up, w_down, g_final)


def _prep_w_in(w):
    lat = Q_LORA + KV_LORA
    kr = w[:, lat:lat + QK_ROPE]
    fox0 = lat + QK_ROPE
    f = w[:, fox0 + 3 * HEAD_WIDTH:]
    pad = jnp.zeros((w.shape[0], LANES - 3 * FOX_HEADS), w.dtype)
    misc = jnp.concatenate([w[:, :lat], kr, kr, f, f, f, pad], axis=1)
    return jnp.concatenate([misc, w[:, fox0:fox0 + 3 * HEAD_WIDTH]], axis=1).astype(_BF16)


def _prep_w_uq(w):
    w3 = w.reshape(Q_LORA, MLA_HEADS, QK_NOPE + QK_ROPE)
    w3 = jnp.concatenate([w3, w3[:, :, QK_NOPE:]], axis=2)
    return w3.reshape(Q_LORA, MLA_HEADS * QK_WIDTH).astype(_BF16)


def kernel(x, positions, g_attn_norm, w_in, b_forget, g_q_lat, w_uq, g_kv_lat, w_ukv, g_out_mla, g_out_fox, w_out, g_ffn_norm, w_gate, w_up, w_down, g_final_norm):
    B, S, D = x.shape
    T = B * S
    depth = w_in.shape[0]
    inv_freq = ROPE_THETA ** (-np.arange(0, QK_ROPE, 2, dtype=np.float32) / QK_ROPE)
    invf_row = jnp.asarray(np.tile(inv_freq.astype(np.float32), LANES // (QK_ROPE // 2))[None, :])
    pos_f = positions.astype(_F32).reshape(T, 1)

    x2d = x.reshape(T, D)
    for l in range(depth):
        b_row = jnp.concatenate(
            [b_forget[l]] * 3 + [jnp.zeros((LANES - 3 * FOX_HEADS,), _F32)]).reshape(1, LANES)
        misc, fq, fk, fv = _in_proj(x2d, g_attn_norm[l].reshape(1, D), _prep_w_in(w_in[l]), b_row, S)
        mq, mk, mv = _mla_up(misc, pos_f, invf_row, g_q_lat[l].reshape(1, Q_LORA),
                             g_kv_lat[l].reshape(1, KV_LORA), _prep_w_uq(w_uq[l]),
                             w_ukv[l].astype(_BF16))
        o_mla = _attention(mq, mk, mv, B, S, CHUNK, "mla_attn")
        o_fox = _attention(fq, fk, fv, B, S, 1, "fox_attn")
        x1 = _out_proj(o_mla, o_fox, g_out_mla[l].reshape(1, HEAD_WIDTH),
                       g_out_fox[l].reshape(1, HEAD_WIDTH), w_out[l].astype(_BF16), x2d)
        g_last = g_final_norm if l == depth - 1 else None
        assert g_last is not None, "only DEPTH == 1 is supported"
        x2d = _ffn(x1, g_ffn_norm[l].reshape(1, D), w_gate[l].astype(_BF16), w_up[l].astype(_BF16),
                   w_down[l].astype(_BF16), g_last.reshape(1, D))
    return x2d.reshape(B, S, D)
```

```python
import functools
import math

import numpy as np
import jax
import jax.numpy as jnp
from jax import lax
from jax.experimental import pallas as pl
from jax.experimental.pallas import tpu as pltpu

D_MODEL = 2048
CHUNK = 64
EPS = 1e-6
ROPE_THETA = 10000.0
MLA_HEADS = 8
Q_LORA = 512
KV_LORA = 256
QK_NOPE = 128
QK_ROPE = 64
V_HEAD = 128
FOX_HEADS = 8
FOX_HEAD_DIM = 128
HEAD_WIDTH = 1024
FFN_HIDDEN = 5632

LANES = 128
QK_WIDTH = 256
LOG2E = math.log2(math.e)
MASK_VALUE = -1e30

IN_TM = 512
IN_TN = 1024
UP_TM = 512
ATT_T = 512
OUT_TM = 512
FFN_TM = 512
FFN_TH = 512

_BF16 = jnp.bfloat16
_F32 = jnp.float32


def _rms_scale(x):
    return x * lax.rsqrt(jnp.mean(x * x, axis=-1, keepdims=True) + EPS)


def _split3_bf16(x):
    hi = x.astype(_BF16)
    r = x - hi.astype(_F32)
    mid = r.astype(_BF16)
    lo = (r - mid.astype(_F32)).astype(_BF16)
    return hi, mid, lo


def _in_proj_kernel(x_ref, g_ref, w_ref, b_ref, misc_ref, q_ref, k_ref, v_ref,
                    h_scr, carry_scr, dec_scr, *, tiles_per_seq, q_scale):
    i = pl.program_id(0)
    j = pl.program_id(1)
    tm = x_ref.shape[0]

    @pl.when(j == 0)
    def _():
        h_scr[...] = (_rms_scale(x_ref[...]) * g_ref[...]).astype(_BF16)

    acc = jnp.dot(h_scr[...], w_ref[...], preferred_element_type=_F32)
    lane = lax.broadcasted_iota(jnp.int32, (tm, LANES), 1)

    @pl.when(j == 0)
    def _():
        misc_ref[...] = acc.astype(_BF16)
        f = acc[:, IN_TN - LANES:] + b_ref[...]
        log_f = jnp.minimum(f, 0.0) - jnp.log1p(jnp.exp(-jnp.abs(f)))
        r = lax.broadcasted_iota(jnp.int32, (tm, tm), 0)
        c = lax.broadcasted_iota(jnp.int32, (tm, tm), 1)
        tri = jnp.where(c <= r, 1.0, 0.0).astype(_BF16)
        hi, mid, lo = _split3_bf16(log_f)
        cum = (jnp.dot(tri, hi, preferred_element_type=_F32)
               + jnp.dot(tri, mid, preferred_element_type=_F32)
               + jnp.dot(tri, lo, preferred_element_type=_F32))

        @pl.when(i % tiles_per_seq == 0)
        def _():
            carry_scr[...] = jnp.zeros_like(carry_scr)

        cum = cum + carry_scr[...]
        carry_scr[...] = cum[tm - 1:tm, :]
        dhi, dmid, dlo = _split3_bf16(cum * (-LOG2E))
        zero = jnp.zeros_like(dhi)
        dec_scr[...] = jnp.where(lane < 8, dhi, jnp.where(lane < 16, dmid, jnp.where(lane < 24, dlo, zero)))

    @pl.when(j == 1)
    def _():
        for h in range(FOX_HEADS):
            q_ref[h, :, 0:LANES] = (acc[:, h * LANES:(h + 1) * LANES] * q_scale).astype(_BF16)
            pick = (lane == h) | (lane == 8 + h) | (lane == 16 + h)
            q_ref[h, :, LANES:QK_WIDTH] = jnp.where(pick, 1.0, 0.0).astype(_BF16)

    @pl.when(j == 2)
    def _():
        for h in range(FOX_HEADS):
            k_ref[h, :, 0:LANES] = acc[:, h * LANES:(h + 1) * LANES].astype(_BF16)
            k_ref[h, :, LANES:QK_WIDTH] = dec_scr[...]

    @pl.when(j == 3)
    def _():
        for h in range(FOX_HEADS):
            v_ref[h, :, 0:LANES] = acc[:, h * LANES:(h + 1) * LANES].astype(_BF16)
            v_ref[h, :, LANES:2 * LANES] = jnp.ones((tm, LANES), _BF16)


def _in_proj(x2d, g, w_in_r, b_row, seq):
    T = x2d.shape[0]
    tm = IN_TM
    grid = (T // tm, w_in_r.shape[1] // IN_TN)
    kern = functools.partial(_in_proj_kernel, tiles_per_seq=seq // tm,
                             q_scale=LOG2E / math.sqrt(FOX_HEAD_DIM))
    return pl.pallas_call(
        kern,
        grid=grid,
        in_specs=[
            pl.BlockSpec((tm, D_MODEL), lambda i, j: (i, 0)),
            pl.BlockSpec((1, D_MODEL), lambda i, j: (0, 0)),
            pl.BlockSpec((D_MODEL, IN_TN), lambda i, j: (0, j)),
            pl.BlockSpec((1, LANES), lambda i, j: (0, 0)),
        ],
        out_specs=[
            pl.BlockSpec((tm, IN_TN), lambda i, j: (i, 0)),
            pl.BlockSpec((FOX_HEADS, tm, QK_WIDTH), lambda i, j: (0, i, 0)),
            pl.BlockSpec((FOX_HEADS, tm, QK_WIDTH), lambda i, j: (0, i, 0)),
            pl.BlockSpec((FOX_HEADS, tm, 2 * LANES), lambda i, j: (0, i, 0)),
        ],
        out_shape=[
            jax.ShapeDtypeStruct((T, IN_TN), _BF16),
            jax.ShapeDtypeStruct((FOX_HEADS, T, QK_WIDTH), _BF16),
            jax.ShapeDtypeStruct((FOX_HEADS, T, QK_WIDTH), _BF16),
            jax.ShapeDtypeStruct((FOX_HEADS, T, 2 * LANES), _BF16),
        ],
        scratch_shapes=[
            pltpu.VMEM((tm, D_MODEL), _BF16),
            pltpu.VMEM((1, LANES), _F32),
            pltpu.VMEM((tm, LANES), _BF16),
        ],
        compiler_params=pltpu.CompilerParams(
            dimension_semantics=("arbitrary", "arbitrary"),
            vmem_limit_bytes=48 * 1024 * 1024),
        name="in_proj",
    )(x2d, g, w_in_r, b_row)


def _mla_up_kernel(misc_ref, pos_ref, invf_ref, gq_ref, gkv_ref, wuq_ref, wukv_ref,
                   q_ref, k_ref, v_ref, *, q_scale):
    tm = misc_ref.shape[0]
    q_lat = misc_ref[:, 0:Q_LORA].astype(_F32)
    kv_lat = misc_ref[:, Q_LORA:Q_LORA + KV_LORA].astype(_F32)
    k_rope = misc_ref[:, Q_LORA + KV_LORA:Q_LORA + KV_LORA + LANES].astype(_F32)

    qn = (_rms_scale(q_lat) * gq_ref[...]).astype(_BF16)
    kvn = (_rms_scale(kv_lat) * gkv_ref[...]).astype(_BF16)
    q = jnp.dot(qn, wuq_ref[...], preferred_element_type=_F32)
    kv = jnp.dot(kvn, wukv_ref[...], preferred_element_type=_F32)

    ang = pos_ref[...] * invf_ref[...]
    cos = jnp.cos(ang)
    sin = jnp.sin(ang)
    lane = lax.broadcasted_iota(jnp.int32, (tm, LANES), 1)
    half = QK_ROPE // 2
    sin_signed = jnp.where((lane % QK_ROPE) < half, -sin, sin)

    def rope(p):
        return p * cos + pltpu.roll(p, half, 1) * sin_signed

    k_pe = jnp.where(lane < QK_ROPE, rope(k_rope), 0.0).astype(_BF16)
    for h in range(MLA_HEADS):
        base = h * QK_WIDTH
        q_ref[h, :, 0:LANES] = (q[:, base:base + LANES] * q_scale).astype(_BF16)
        q_ref[h, :, LANES:QK_WIDTH] = (rope(q[:, base + LANES:base + QK_WIDTH]) * q_scale).astype(_BF16)
        k_ref[h, :, 0:LANES] = kv[:, base:base + LANES].astype(_BF16)
        k_ref[h, :, LANES:QK_WIDTH] = k_pe
        v_ref[h, :, 0:LANES] = kv[:, base + LANES:base + QK_WIDTH].astype(_BF16)
        v_ref[h, :, LANES:2 * LANES] = jnp.ones((tm, LANES), _BF16)


def _mla_up(misc, pos_f, invf_row, gq, gkv, w_uq_r, w_ukv):
    T = misc.shape[0]
    tm = UP_TM
    kern = functools.partial(_mla_up_kernel, q_scale=LOG2E / math.sqrt(QK_NOPE + QK_ROPE))
    return pl.pallas_call(
        kern,
        grid=(T // tm,),
        in_specs=[
            pl.BlockSpec((tm, IN_TN), lambda i: (i, 0)),
            pl.BlockSpec((tm, 1), lambda i: (i, 0)),
            pl.BlockSpec((1, LANES), lambda i: (0, 0)),
            pl.BlockSpec((1, Q_LORA), lambda i: (0, 0)),
            pl.BlockSpec((1, KV_LORA), lambda i: (0, 0)),
            pl.BlockSpec((Q_LORA, MLA_HEADS * QK_WIDTH), lambda i: (0, 0)),
            pl.BlockSpec((KV_LORA, MLA_HEADS * QK_WIDTH), lambda i: (0, 0)),
        ],
        out_specs=[
            pl.BlockSpec((MLA_HEADS, tm, QK_WIDTH), lambda i: (0, i, 0)),
            pl.BlockSpec((MLA_HEADS, tm, QK_WIDTH), lambda i: (0, i, 0)),
            pl.BlockSpec((MLA_HEADS, tm, 2 * LANES), lambda i: (0, i, 0)),
        ],
        out_shape=[
            jax.ShapeDtypeStruct((MLA_HEADS, T, QK_WIDTH), _BF16),
            jax.ShapeDtypeStruct((MLA_HEADS, T, QK_WIDTH), _BF16),
            jax.ShapeDtypeStruct((MLA_HEADS, T, 2 * LANES), _BF16),
        ],
        compiler_params=pltpu.CompilerParams(
            dimension_semantics=("arbitrary",),
            vmem_limit_bytes=40 * 1024 * 1024),
        name="mla_up",
    )(misc, pos_f, invf_row, gq, gkv, w_uq_r, w_ukv)


def _attn_kernel(q_ref, k_ref, v_ref, o_ref, s0_scr, s1_scr, m_scr, acc_scr, *, causal_unit):
    seq = q_ref.shape[0]
    t = ATT_T
    nt = seq // t
    row = lax.broadcasted_iota(jnp.int32, (t, t), 0)
    col = lax.broadcasted_iota(jnp.int32, (t, t), 1)
    if causal_unit == 1:
        allowed = col <= row
    else:
        allowed = (col // causal_unit) <= (row // causal_unit)

    def scores(q, ks):
        k = k_ref[pl.ds(ks, t), :]
        return lax.dot_general(q, k, (((1,), (1,)), ((), ())), preferred_element_type=_F32)

    def update(s, ks):
        m = m_scr[...]
        m_new = jnp.maximum(m, jnp.max(s, axis=1, keepdims=True))
        alpha = jnp.exp2(m - m_new)
        m_scr[...] = m_new
        p = jnp.concatenate(
            [jnp.exp2(s[:, c * LANES:(c + 1) * LANES] - m_new) for c in range(t // LANES)], axis=1)
        v = v_ref[pl.ds(ks, t), :]
        pv = jnp.dot(p.astype(_BF16), v, preferred_element_type=_F32)
        for c in range(2):
            sl = slice(c * LANES, (c + 1) * LANES)
            acc_scr[:, sl] = alpha * acc_scr[:, sl] + pv[:, sl]

    s_scr = (s0_scr, s1_scr)

    for i in range(nt):
        qs = i * t
        q = q_ref[qs:qs + t, :]
        m_scr[...] = jnp.full(m_scr.shape, MASK_VALUE, _F32)
        acc_scr[...] = jnp.zeros_like(acc_scr)
        s_scr[0][...] = scores(q, 0)
        for j in range(i + 1):
            cur = s_scr[j % 2]
            if j < i:
                s_scr[(j + 1) % 2][...] = scores(q, (j + 1) * t)
                update(cur[...], j * t)
            else:
                update(jnp.where(allowed, cur[...], MASK_VALUE), j * t)
        o_ref[qs:qs + t, :] = (acc_scr[:, 0:LANES] / acc_scr[:, LANES:2 * LANES]).astype(o_ref.dtype)


def _attention(q, k, v, batch, seq, causal_unit, name):
    heads = q.shape[0]
    kern = functools.partial(_attn_kernel, causal_unit=causal_unit)
    return pl.pallas_call(
        kern,
        grid=(heads, batch),
        in_specs=[
            pl.BlockSpec((None, seq, QK_WIDTH), lambda h, b: (h, b, 0)),
            pl.BlockSpec((None, seq, QK_WIDTH), lambda h, b: (h, b, 0)),
            pl.BlockSpec((None, seq, 2 * LANES), lambda h, b: (h, b, 0)),
        ],
        out_specs=pl.BlockSpec((seq, LANES), lambda h, b: (b, h)),
        out_shape=jax.ShapeDtypeStruct((batch * seq, heads * LANES), _BF16),
        scratch_shapes=[
            pltpu.VMEM((ATT_T, ATT_T), _F32),
            pltpu.VMEM((ATT_T, ATT_T), _F32),
            pltpu.VMEM((ATT_T, LANES), _F32),
            pltpu.VMEM((ATT_T, 2 * LANES), _F32),
        ],
        compiler_params=pltpu.CompilerParams(
            dimension_semantics=("arbitrary", "arbitrary"),
            vmem_limit_bytes=40 * 1024 * 1024),
        name=name,
    )(q, k, v)


def _out_proj_kernel(om_ref, of_ref, gm_ref, gf_ref, w_ref, x_ref, o_ref):
    om = (_rms_scale(om_ref[...].astype(_F32)) * gm_ref[...]).astype(_BF16)
    of = (_rms_scale(of_ref[...].astype(_F32)) * gf_ref[...]).astype(_BF16)
    y = jnp.dot(om, w_ref[0:HEAD_WIDTH, :], preferred_element_type=_F32)
    y = y + jnp.dot(of, w_ref[HEAD_WIDTH:2 * HEAD_WIDTH, :], preferred_element_type=_F32)
    o_ref[...] = x_ref[...] + y


def _out_proj(o_mla, o_fox, gm, gf, w_out, x2d):
    T = x2d.shape[0]
    tm = OUT_TM
    return pl.pallas_call(
        _out_proj_kernel,
        grid=(T // tm,),
        in_specs=[
            pl.BlockSpec((tm, HEAD_WIDTH), lambda i: (i, 0)),
            pl.BlockSpec((tm, HEAD_WIDTH), lambda i: (i, 0)),
            pl.BlockSpec((1, HEAD_WIDTH), lambda i: (0, 0)),
            pl.BlockSpec((1, HEAD_WIDTH), lambda i: (0, 0)),
            pl.BlockSpec((2 * HEAD_WIDTH, D_MODEL), lambda i: (0, 0)),
            pl.BlockSpec((tm, D_MODEL), lambda i: (i, 0)),
        ],
        out_specs=pl.BlockSpec((tm, D_MODEL), lambda i: (i, 0)),
        out_shape=jax.ShapeDtypeStruct((T, D_MODEL), _F32),
        compiler_params=pltpu.CompilerParams(
            dimension_semantics=("arbitrary",),
            vmem_limit_bytes=48 * 1024 * 1024),
        name="out_proj",
    )(o_mla, o_fox, gm, gf, w_out, x2d)


def _ffn_kernel(x_ref, g_ref, wg_ref, wu_ref, wd_ref, gfin_ref, o_ref, h_scr, acc_scr):
    j = pl.program_id(1)

    @pl.when(j == 0)
    def _():
        h_scr[...] = (_rms_scale(x_ref[...]) * g_ref[...]).astype(_BF16)
        acc_scr[...] = jnp.zeros_like(acc_scr)

    h = h_scr[...]
    gate = jnp.dot(h, wg_ref[...], preferred_element_type=_F32)
    up = jnp.dot(h, wu_ref[...], preferred_element_type=_F32)
    a = (gate * (1.0 / (1.0 + jnp.exp(-gate))) * up).astype(_BF16)
    acc_scr[...] += jnp.dot(a, wd_ref[...], preferred_element_type=_F32)

    @pl.when(j == pl.num_programs(1) - 1)
    def _():
        y = x_ref[...] + acc_scr[...]
        o_ref[...] = _rms_scale(y) * gfin_ref[...]


def _ffn(x1, g, w_gate, w_up, w_down, g_final):
    T = x1.shape[0]
    tm, th = FFN_TM, FFN_TH
    return pl.pallas_call(
        _ffn_kernel,
        grid=(T // tm, FFN_HIDDEN // th),
        in_specs=[
            pl.BlockSpec((tm, D_MODEL), lambda i, j: (i, 0)),
            pl.BlockSpec((1, D_MODEL), lambda i, j: (0, 0)),
            pl.BlockSpec((D_MODEL, th), lambda i, j: (0, j)),
            pl.BlockSpec((D_MODEL, th), lambda i, j: (0, j)),
            pl.BlockSpec((th, D_MODEL), lambda i, j: (j, 0)),
            pl.BlockSpec((1, D_MODEL), lambda i, j: (0, 0)),
        ],
        out_specs=pl.BlockSpec((tm, D_MODEL), lambda i, j: (i, 0)),
        out_shape=jax.ShapeDtypeStruct((T, D_MODEL), _F32),
        scratch_shapes=[
            pltpu.VMEM((tm, D_MODEL), _BF16),
            pltpu.VMEM((tm, D_MODEL), _F32),
        ],
        compiler_params=pltpu.CompilerParams(
            dimension_semantics=("arbitrary", "arbitrary"),
            vmem_limit_bytes=48 * 1024 * 1024),
        name="ffn",
    )(x1, g, w_gate, w_up, w_down, g_final)


def _prep_w_in(w):
    lat = Q_LORA + KV_LORA
    kr = w[:, lat:lat + QK_ROPE]
    fox0 = lat + QK_ROPE
    f = w[:, fox0 + 3 * HEAD_WIDTH:]
    pad = jnp.zeros((w.shape[0], LANES - 3 * FOX_HEADS), w.dtype)
    misc = jnp.concatenate([w[:, :lat], kr, kr, f, f, f, pad], axis=1)
    return jnp.concatenate([misc, w[:, fox0:fox0 + 3 * HEAD_WIDTH]], axis=1).astype(_BF16)


def _prep_w_uq(w):
    w3 = w.reshape(Q_LORA, MLA_HEADS, QK_NOPE + QK_ROPE)
    w3 = jnp.concatenate([w3, w3[:, :, QK_NOPE:]], axis=2)
    return w3.reshape(Q_LORA, MLA_HEADS * QK_WIDTH).astype(_BF16)


def kernel(x, positions, g_attn_norm, w_in, b_forget, g_q_lat, w_uq, g_kv_lat, w_ukv, g_out_mla, g_out_fox, w_out, g_ffn_norm, w_gate, w_up, w_down, g_final_norm):
    B, S, D = x.shape
    T = B * S
    depth = w_in.shape[0]
    inv_freq = ROPE_THETA ** (-np.arange(0, QK_ROPE, 2, dtype=np.float32) / QK_ROPE)
    invf_row = jnp.asarray(np.tile(inv_freq.astype(np.float32), LANES // (QK_ROPE // 2))[None, :])
    pos_f = positions.astype(_F32).reshape(T, 1)

    x2d = x.reshape(T, D)
    for l in range(depth):
        b_row = jnp.concatenate(
            [b_forget[l]] * 3 + [jnp.zeros((LANES - 3 * FOX_HEADS,), _F32)]).reshape(1, LANES)
        misc, fq, fk, fv = _in_proj(x2d, g_attn_norm[l].reshape(1, D), _prep_w_in(w_in[l]), b_row, S)
        mq, mk, mv = _mla_up(misc, pos_f, invf_row, g_q_lat[l].reshape(1, Q_LORA),
                             g_kv_lat[l].reshape(1, KV_LORA), _prep_w_uq(w_uq[l]),
                             w_ukv[l].astype(_BF16))
        o_mla = _attention(mq, mk, mv, B, S, CHUNK, "mla_attn")
        o_fox = _attention(fq, fk, fv, B, S, 1, "fox_attn")
        x1 = _out_proj(o_mla, o_fox, g_out_mla[l].reshape(1, HEAD_WIDTH),
                       g_out_fox[l].reshape(1, HEAD_WIDTH), w_out[l].astype(_BF16), x2d)
        g_last = g_final_norm if l == depth - 1 else None
        assert g_last is not None, "only DEPTH == 1 is supported"
        x2d = _ffn(x1, g_ffn_norm[l].reshape(1, D), w_gate[l].astype(_BF16), w_up[l].astype(_BF16),
                   w_down[l].astype(_BF16), g_last.reshape(1, D))
    return x2d.reshape(B, S, D)
```

```python
import functools
import math

import jax
import jax.numpy as jnp
from jax import lax
from jax.experimental import pallas as pl
from jax.experimental.pallas import tpu as pltpu

D_MODEL = 2048
CHUNK = 64
EPS = 1e-6
ROPE_THETA = 10000.0
MLA_HEADS = 8
Q_LORA = 512
KV_LORA = 256
QK_NOPE = 128
QK_ROPE = 64
V_HEAD = 128
FOX_HEADS = 8
FOX_HEAD_DIM = 128
HEAD_WIDTH = 1024
FFN_HIDDEN = 5632

LANES = 128
QK_WIDTH = 256
LOG2E = math.log2(math.e)
MASK_VALUE = -1e30

IN_TM = 512
IN_TN = 1024
CUM_ROWS = 256
UP_TM = 512
ATT_T = 512
OUT_TM = 512
FFN_TM = 512
FFN_TH = 512

_BF16 = jnp.bfloat16
_F32 = jnp.float32


def _rms_scale(x):
    return x * lax.rsqrt(jnp.mean(x * x, axis=-1, keepdims=True) + EPS)


def _split3_bf16(x):
    hi = x.astype(_BF16)
    r = x - hi.astype(_F32)
    mid = r.astype(_BF16)
    lo = (r - mid.astype(_F32)).astype(_BF16)
    return hi, mid, lo


def _in_proj_kernel(x_ref, g_ref, wm_ref, wf_ref, b_ref, pos_ref, invf_ref,
                    misc_ref, q_ref, k_ref, v_ref, rot_ref,
                    h_scr, carry_scr, dec_scr, *, tiles_per_seq, q_scale):
    i = pl.program_id(0)
    j = pl.program_id(1)
    tm = x_ref.shape[0]
    lane = lax.broadcasted_iota(jnp.int32, (tm, LANES), 1)

    @pl.when(j == 0)
    def _():
        h = (_rms_scale(x_ref[...]) * g_ref[...]).astype(_BF16)
        h_scr[...] = h
        acc = jnp.dot(h, wm_ref[...], preferred_element_type=_F32)
        misc_ref[...] = acc.astype(_BF16)
        f = acc[:, IN_TN - LANES:] + b_ref[...]
        log_f = jnp.minimum(f, 0.0) - jnp.log1p(jnp.exp(-jnp.abs(f)))
        hi, mid, lo = _split3_bf16(log_f)
        zero = jnp.zeros_like(hi)
        parts = jnp.where(lane < 8, hi, jnp.where(lane < 16, mid, jnp.where(lane < 24, lo, zero)))
        r = lax.broadcasted_iota(jnp.int32, (CUM_ROWS, CUM_ROWS), 0)
        c = lax.broadcasted_iota(jnp.int32, (CUM_ROWS, CUM_ROWS), 1)
        tri = jnp.where(c <= r, 1.0, 0.0).astype(_BF16)

        @pl.when(i % tiles_per_seq == 0)
        def _():
            carry_scr[...] = jnp.zeros_like(carry_scr)

        carry = carry_scr[...]
        chunks = []
        for ch in range(tm // CUM_ROWS):
            y = jnp.dot(tri, parts[ch * CUM_ROWS:(ch + 1) * CUM_ROWS], preferred_element_type=_F32)
            y = (y + pltpu.roll(y, 8, 1) + pltpu.roll(y, 16, 1)
                 + pltpu.roll(y, LANES - 8, 1) + pltpu.roll(y, LANES - 16, 1)) + carry
            carry = y[CUM_ROWS - 1:CUM_ROWS, :]
            chunks.append(y)
        carry_scr[...] = carry
        cum = jnp.concatenate(chunks, axis=0)
        dhi, dmid, dlo = _split3_bf16(cum * (-LOG2E))
        dec_scr[...] = jnp.where(lane < 8, dhi, jnp.where(lane < 16, dmid, jnp.where(lane < 24, dlo, zero)))

    @pl.when(j == 1)
    def _():
        acc = jnp.dot(h_scr[...], wf_ref[...], preferred_element_type=_F32)
        for h in range(FOX_HEADS):
            q_ref[h, :, 0:LANES] = (acc[:, h * LANES:(h + 1) * LANES] * q_scale).astype(_BF16)
            pick = (lane == h) | (lane == 8 + h) | (lane == 16 + h)
            q_ref[h, :, LANES:QK_WIDTH] = jnp.where(pick, 1.0, 0.0).astype(_BF16)
        ang = pos_ref[...] * invf_ref[...]
        sin = jnp.sin(ang)
        rot_ref[:, 0:LANES] = jnp.cos(ang)
        rot_ref[:, LANES:2 * LANES] = jnp.where((lane % QK_ROPE) < QK_ROPE // 2, -sin, sin)

    @pl.when(j == 2)
    def _():
        acc = jnp.dot(h_scr[...], wf_ref[...], preferred_element_type=_F32)
        for h in range(FOX_HEADS):
            k_ref[h, :, 0:LANES] = acc[:, h * LANES:(h + 1) * LANES].astype(_BF16)
            k_ref[h, :, LANES:QK_WIDTH] = dec_scr[...]

    @pl.when(j == 3)
    def _():
        acc = jnp.dot(h_scr[...], wf_ref[...], preferred_element_type=_F32)
        for h in range(FOX_HEADS):
            v_ref[h, :, 0:LANES] = acc[:, h * LANES:(h + 1) * LANES].astype(_BF16)
            v_ref[h, :, LANES:2 * LANES] = jnp.ones((tm, LANES), _BF16)


def _in_proj(x2d, g, w_misc, w_fox, b_row, pos_f, invf_row, seq):
    T = x2d.shape[0]
    tm = IN_TM
    grid = (T // tm, 1 + w_fox.shape[1] // IN_TN)
    kern = functools.partial(_in_proj_kernel, tiles_per_seq=seq // tm,
                             q_scale=LOG2E / math.sqrt(FOX_HEAD_DIM))
    return pl.pallas_call(
        kern,
        grid=grid,
        in_specs=[
            pl.BlockSpec((tm, D_MODEL), lambda i, j: (i, 0)),
            pl.BlockSpec((1, D_MODEL), lambda i, j: (0, 0)),
            pl.BlockSpec((D_MODEL, IN_TN), lambda i, j: (0, 0)),
            pl.BlockSpec((D_MODEL, IN_TN), lambda i, j: (0, jnp.maximum(j - 1, 0))),
            pl.BlockSpec((1, LANES), lambda i, j: (0, 0)),
            pl.BlockSpec((tm, 1), lambda i, j: (i, 0)),
            pl.BlockSpec((1, LANES), lambda i, j: (0, 0)),
        ],
        out_specs=[
            pl.BlockSpec((tm, IN_TN), lambda i, j: (i, 0)),
            pl.BlockSpec((FOX_HEADS, tm, QK_WIDTH), lambda i, j: (0, i, 0)),
            pl.BlockSpec((FOX_HEADS, tm, QK_WIDTH), lambda i, j: (0, i, 0)),
            pl.BlockSpec((FOX_HEADS, tm, 2 * LANES), lambda i, j: (0, i, 0)),
            pl.BlockSpec((tm, 2 * LANES), lambda i, j: (i, 0)),
        ],
        out_shape=[
            jax.ShapeDtypeStruct((T, IN_TN), _BF16),
            jax.ShapeDtypeStruct((FOX_HEADS, T, QK_WIDTH), _BF16),
            jax.ShapeDtypeStruct((FOX_HEADS, T, QK_WIDTH), _BF16),
            jax.ShapeDtypeStruct((FOX_HEADS, T, 2 * LANES), _BF16),
            jax.ShapeDtypeStruct((T, 2 * LANES), _F32),
        ],
        scratch_shapes=[
            pltpu.VMEM((tm, D_MODEL), _BF16),
            pltpu.VMEM((1, LANES), _F32),
            pltpu.VMEM((tm, LANES), _BF16),
        ],
        compiler_params=pltpu.CompilerParams(
            dimension_semantics=("arbitrary", "arbitrary"),
            vmem_limit_bytes=52 * 1024 * 1024),
        name="in_proj",
    )(x2d, g, w_misc, w_fox, b_row, pos_f, invf_row)


def _mla_up_kernel(misc_ref, rot_ref, gq_ref, gkv_ref, wuq_ref, wukv_ref,
                   q_ref, k_ref, v_ref, *, q_scale):
    tm = misc_ref.shape[0]
    q_lat = misc_ref[:, 0:Q_LORA].astype(_F32)
    kv_lat = misc_ref[:, Q_LORA:Q_LORA + KV_LORA].astype(_F32)
    k_rope = misc_ref[:, Q_LORA + KV_LORA:Q_LORA + KV_LORA + LANES].astype(_F32)

    qn = (_rms_scale(q_lat) * gq_ref[...]).astype(_BF16)
    kvn = (_rms_scale(kv_lat) * gkv_ref[...]).astype(_BF16)
    q = jnp.dot(qn, wuq_ref[...], preferred_element_type=_F32)
    kv = jnp.dot(kvn, wukv_ref[...], preferred_element_type=_F32)

    cos = rot_ref[:, 0:LANES]
    sin_signed = rot_ref[:, LANES:2 * LANES]
    lane = lax.broadcasted_iota(jnp.int32, (tm, LANES), 1)

    def rope(p):
        return p * cos + pltpu.roll(p, QK_ROPE // 2, 1) * sin_signed

    k_pe = jnp.where(lane < QK_ROPE, rope(k_rope), 0.0).astype(_BF16)
    for h in range(MLA_HEADS):
        base = h * QK_WIDTH
        q_ref[h, :, 0:LANES] = (q[:, base:base + LANES] * q_scale).astype(_BF16)
        q_ref[h, :, LANES:QK_WIDTH] = (rope(q[:, base + LANES:base + QK_WIDTH]) * q_scale).astype(_BF16)
        k_ref[h, :, 0:LANES] = kv[:, base:base + LANES].astype(_BF16)
        k_ref[h, :, LANES:QK_WIDTH] = k_pe
        v_ref[h, :, 0:LANES] = kv[:, base + LANES:base + QK_WIDTH].astype(_BF16)
        v_ref[h, :, LANES:2 * LANES] = jnp.ones((tm, LANES), _BF16)


def _mla_up(misc, rot, gq, gkv, w_uq_r, w_ukv):
    T = misc.shape[0]
    tm = UP_TM
    kern = functools.partial(_mla_up_kernel, q_scale=LOG2E / math.sqrt(QK_NOPE + QK_ROPE))
    return pl.pallas_call(
        kern,
        grid=(T // tm,),
        in_specs=[
            pl.BlockSpec((tm, IN_TN), lambda i: (i, 0)),
            pl.BlockSpec((tm, 2 * LANES), lambda i: (i, 0)),
            pl.BlockSpec((1, Q_LORA), lambda i: (0, 0)),
            pl.BlockSpec((1, KV_LORA), lambda i: (0, 0)),
            pl.BlockSpec((Q_LORA, MLA_HEADS * QK_WIDTH), lambda i: (0, 0)),
            pl.BlockSpec((KV_LORA, MLA_HEADS * QK_WIDTH), lambda i: (0, 0)),
        ],
        out_specs=[
            pl.BlockSpec((MLA_HEADS, tm, QK_WIDTH), lambda i: (0, i, 0)),
            pl.BlockSpec((MLA_HEADS, tm, QK_WIDTH), lambda i: (0, i, 0)),
            pl.BlockSpec((MLA_HEADS, tm, 2 * LANES), lambda i: (0, i, 0)),
        ],
        out_shape=[
            jax.ShapeDtypeStruct((MLA_HEADS, T, QK_WIDTH), _BF16),
            jax.ShapeDtypeStruct((MLA_HEADS, T, QK_WIDTH), _BF16),
            jax.ShapeDtypeStruct((MLA_HEADS, T, 2 * LANES), _BF16),
        ],
        compiler_params=pltpu.CompilerParams(
            dimension_semantics=("arbitrary",),
            vmem_limit_bytes=40 * 1024 * 1024),
        name="mla_up",
    )(misc, rot, gq, gkv, w_uq_r, w_ukv)


def _attn_kernel(q_ref, k_ref, v_ref, o_ref, s0_scr, s1_scr, m_scr, acc_scr, *, causal_unit):
    seq = q_ref.shape[0]
    t = ATT_T
    nt = seq // t
    row = lax.broadcasted_iota(jnp.int32, (t, t), 0)
    col = lax.broadcasted_iota(jnp.int32, (t, t), 1)
    if causal_unit == 1:
        allowed = col <= row
    else:
        allowed = (col // causal_unit) <= (row // causal_unit)

    def scores(q, ks):
        k = k_ref[pl.ds(ks, t), :]
        return lax.dot_general(q, k, (((1,), (1,)), ((), ())), preferred_element_type=_F32)

    def update(s, ks):
        m = m_scr[...]
        m_new = jnp.maximum(m, jnp.max(s, axis=1, keepdims=True))
        alpha = jnp.exp2(m - m_new)
        m_scr[...] = m_new
        p = jnp.concatenate(
            [jnp.exp2(s[:, c * LANES:(c + 1) * LANES] - m_new) for c in range(t // LANES)], axis=1)
        v = v_ref[pl.ds(ks, t), :]
        pv = jnp.dot(p.astype(_BF16), v, preferred_element_type=_F32)
        for c in range(2):
            sl = slice(c * LANES, (c + 1) * LANES)
            acc_scr[:, sl] = alpha * acc_scr[:, sl] + pv[:, sl]

    s_scr = (s0_scr, s1_scr)

    for i in range(nt):
        qs = i * t
        q = q_ref[qs:qs + t, :]
        m_scr[...] = jnp.full(m_scr.shape, MASK_VALUE, _F32)
        acc_scr[...] = jnp.zeros_like(acc_scr)
        s_scr[0][...] = scores(q, 0)
        for j in range(i + 1):
            cur = s_scr[j % 2]
            if j < i:
                s_scr[(j + 1) % 2][...] = scores(q, (j + 1) * t)
                update(cur[...], j * t)
            else:
                update(jnp.where(allowed, cur[...], MASK_VALUE), j * t)
        o_ref[qs:qs + t, :] = (acc_scr[:, 0:LANES] / acc_scr[:, LANES:2 * LANES]).astype(o_ref.dtype)


def _attention(q, k, v, batch, seq, causal_unit, name):
    heads = q.shape[0]
    kern = functools.partial(_attn_kernel, causal_unit=causal_unit)
    return pl.pallas_call(
        kern,
        grid=(heads, batch),
        in_specs=[
            pl.BlockSpec((None, seq, QK_WIDTH), lambda h, b: (h, b, 0)),
            pl.BlockSpec((None, seq, QK_WIDTH), lambda h, b: (h, b, 0)),
            pl.BlockSpec((None, seq, 2 * LANES), lambda h, b: (h, b, 0)),
        ],
        out_specs=pl.BlockSpec((seq, LANES), lambda h, b: (b, h)),
        out_shape=jax.ShapeDtypeStruct((batch * seq, heads * LANES), _BF16),
        scratch_shapes=[
            pltpu.VMEM((ATT_T, ATT_T), _F32),
            pltpu.VMEM((ATT_T, ATT_T), _F32),
            pltpu.VMEM((ATT_T, LANES), _F32),
            pltpu.VMEM((ATT_T, 2 * LANES), _F32),
        ],
        compiler_params=pltpu.CompilerParams(
            dimension_semantics=("arbitrary", "arbitrary"),
            vmem_limit_bytes=40 * 1024 * 1024),
        name=name,
    )(q, k, v)


def _out_proj_kernel(om_ref, of_ref, gm_ref, gf_ref, w_ref, x_ref, o_ref):
    om = (_rms_scale(om_ref[...].astype(_F32)) * gm_ref[...]).astype(_BF16)
    of = (_rms_scale(of_ref[...].astype(_F32)) * gf_ref[...]).astype(_BF16)
    y = jnp.dot(om, w_ref[0:HEAD_WIDTH, :], preferred_element_type=_F32)
    y = y + jnp.dot(of, w_ref[HEAD_WIDTH:2 * HEAD_WIDTH, :], preferred_element_type=_F32)
    o_ref[...] = x_ref[...] + y


def _out_proj(o_mla, o_fox, gm, gf, w_out, x2d):
    T = x2d.shape[0]
    tm = OUT_TM
    return pl.pallas_call(
        _out_proj_kernel,
        grid=(T // tm,),
        in_specs=[
            pl.BlockSpec((tm, HEAD_WIDTH), lambda i: (i, 0)),
            pl.BlockSpec((tm, HEAD_WIDTH), lambda i: (i, 0)),
            pl.BlockSpec((1, HEAD_WIDTH), lambda i: (0, 0)),
            pl.BlockSpec((1, HEAD_WIDTH), lambda i: (0, 0)),
            pl.BlockSpec((2 * HEAD_WIDTH, D_MODEL), lambda i: (0, 0)),
            pl.BlockSpec((tm, D_MODEL), lambda i: (i, 0)),
        ],
        out_specs=pl.BlockSpec((tm, D_MODEL), lambda i: (i, 0)),
        out_shape=jax.ShapeDtypeStruct((T, D_MODEL), _F32),
        compiler_params=pltpu.CompilerParams(
            dimension_semantics=("arbitrary",),
            vmem_limit_bytes=48 * 1024 * 1024),
        name="out_proj",
    )(o_mla, o_fox, gm, gf, w_out, x2d)


def _ffn_kernel(x_ref, g_ref, wg_ref, wu_ref, wd_ref, gfin_ref, o_ref, h_scr, acc_scr):
    j = pl.program_id(1)

    @pl.when(j == 0)
    def _():
        h_scr[...] = (_rms_scale(x_ref[...]) * g_ref[...]).astype(_BF16)
        acc_scr[...] = jnp.zeros_like(acc_scr)

    h = h_scr[...]
    gate = jnp.dot(h, wg_ref[...], preferred_element_type=_F32)
    up = jnp.dot(h, wu_ref[...], preferred_element_type=_F32)
    a = (gate * (1.0 / (1.0 + jnp.exp(-gate))) * up).astype(_BF16)
    acc_scr[...] += jnp.dot(a, wd_ref[...], preferred_element_type=_F32)

    @pl.when(j == pl.num_programs(1) - 1)
    def _():
        y = x_ref[...] + acc_scr[...]
        o_ref[...] = _rms_scale(y) * gfin_ref[...]


def _ffn(x1, g, w_gate, w_up, w_down, g_final):
    T = x1.shape[0]
    tm, th = FFN_TM, FFN_TH
    return pl.pallas_call(
        _ffn_kernel,
        grid=(T // tm, FFN_HIDDEN // th),
        in_specs=[
            pl.BlockSpec((tm, D_MODEL), lambda i, j: (i, 0)),
            pl.BlockSpec((1, D_MODEL), lambda i, j: (0, 0)),
            pl.BlockSpec((D_MODEL, th), lambda i, j: (0, j)),
            pl.BlockSpec((D_MODEL, th), lambda i, j: (0, j)),
            pl.BlockSpec((th, D_MODEL), lambda i, j: (j, 0)),
            pl.BlockSpec((1, D_MODEL), lambda i, j: (0, 0)),
        ],
        out_specs=pl.BlockSpec((tm, D_MODEL), lambda i, j: (i, 0)),
        out_shape=jax.ShapeDtypeStruct((T, D_MODEL), _F32),
        scratch_shapes=[
            pltpu.VMEM((tm, D_MODEL), _BF16),
            pltpu.VMEM((tm, D_MODEL), _F32),
        ],
        compiler_params=pltpu.CompilerParams(
            dimension_semantics=("arbitrary", "arbitrary"),
            vmem_limit_bytes=48 * 1024 * 1024),
        name="ffn",
    )(x1, g, w_gate, w_up, w_down, g_final)


def _prep_w_in(w):
    w = w.astype(_BF16)
    lat = Q_LORA + KV_LORA
    kr = w[:, lat:lat + QK_ROPE]
    fox0 = lat + QK_ROPE
    f = w[:, fox0 + 3 * HEAD_WIDTH:]
    pad = jnp.zeros((w.shape[0], LANES - 3 * FOX_HEADS), w.dtype)
    misc = jnp.concatenate([w[:, :lat], kr, kr, f, f, f, pad], axis=1)
    return misc, w[:, fox0:fox0 + 3 * HEAD_WIDTH]


def _prep_w_uq(w):
    w3 = w.reshape(Q_LORA, MLA_HEADS, QK_NOPE + QK_ROPE)
    w3 = jnp.concatenate([w3, w3[:, :, QK_NOPE:]], axis=2)
    return w3.reshape(Q_LORA, MLA_HEADS * QK_WIDTH).astype(_BF16)


def kernel(x, positions, g_attn_norm, w_in, b_forget, g_q_lat, w_uq, g_kv_lat, w_ukv, g_out_mla, g_out_fox, w_out, g_ffn_norm, w_gate, w_up, w_down, g_final_norm):
    B, S, D = x.shape
    T = B * S
    assert w_in.shape[0] == 1, "one layer (DEPTH == 1) is supported"
    inv_freq = ROPE_THETA ** (-jnp.arange(0, QK_ROPE, 2, dtype=_F32) / QK_ROPE)
    invf_row = jnp.tile(inv_freq, LANES // (QK_ROPE // 2)).reshape(1, LANES)
    pos_f = positions.astype(_F32).reshape(T, 1)
    x2d = x.reshape(T, D)

    b_row = jnp.concatenate(
        [b_forget[0]] * 3 + [jnp.zeros((LANES - 3 * FOX_HEADS,), _F32)]).reshape(1, LANES)
    w_misc, w_fox = _prep_w_in(w_in[0])
    misc, fq, fk, fv, rot = _in_proj(x2d, g_attn_norm[0].reshape(1, D), w_misc, w_fox, b_row,
                                     pos_f, invf_row, S)
    mq, mk, mv = _mla_up(misc, rot, g_q_lat[0].reshape(1, Q_LORA),
                         g_kv_lat[0].reshape(1, KV_LORA), _prep_w_uq(w_uq[0]), w_ukv[0].astype(_BF16))
    o_mla = _attention(mq, mk, mv, B, S, CHUNK, "mla_attn")
    o_fox = _attention(fq, fk, fv, B, S, 1, "fox_attn")
    x1 = _out_proj(o_mla, o_fox, g_out_mla[0].reshape(1, HEAD_WIDTH),
                   g_out_fox[0].reshape(1, HEAD_WIDTH), w_out[0].astype(_BF16), x2d)
    out = _ffn(x1, g_ffn_norm[0].reshape(1, D), w_gate[0].astype(_BF16), w_up[0].astype(_BF16),
               w_down[0].astype(_BF16), g_final_norm.reshape(1, D))
    return out.reshape(B, S, D)
```

```python
import functools
import math

import jax
import jax.numpy as jnp
from jax import lax
from jax.experimental import pallas as pl
from jax.experimental.pallas import tpu as pltpu

D_MODEL = 2048
CHUNK = 64
EPS = 1e-6
ROPE_THETA = 10000.0
MLA_HEADS = 8
Q_LORA = 512
KV_LORA = 256
QK_NOPE = 128
QK_ROPE = 64
V_HEAD = 128
FOX_HEADS = 8
FOX_HEAD_DIM = 128
HEAD_WIDTH = 1024
FFN_HIDDEN = 5632

LANES = 128
QK_WIDTH = 256
LOG2E = math.log2(math.e)
MASK_VALUE = -1e30

IN_TM = 512
IN_TN = 1024
CUM_ROWS = 256
UP_TM = 512
ATT_T = 512
OUT_TM = 512
FFN_TM = 512
FFN_TH = 512

_BF16 = jnp.bfloat16
_F32 = jnp.float32


def _rms_scale(x):
    return x * lax.rsqrt(jnp.mean(x * x, axis=-1, keepdims=True) + EPS)


def _split3_bf16(x):
    hi = x.astype(_BF16)
    r = x - hi.astype(_F32)
    mid = r.astype(_BF16)
    lo = (r - mid.astype(_F32)).astype(_BF16)
    return hi, mid, lo


def _in_proj_kernel(x_ref, g_ref, wm_ref, wf_ref, b_ref, pos_ref, invf_ref,
                    misc_ref, dec_ref, rot_ref, qkv_ref,
                    h_scr, carry_scr, *, tiles_per_seq, q_scale):
    i = pl.program_id(0)
    j = pl.program_id(1)
    tm = x_ref.shape[0]
    lane = lax.broadcasted_iota(jnp.int32, (tm, LANES), 1)

    @pl.when(j == 0)
    def _():
        h = (_rms_scale(x_ref[...]) * g_ref[...]).astype(_BF16)
        h_scr[...] = h
        acc = jnp.dot(h, wm_ref[...], preferred_element_type=_F32)
        misc_ref[...] = acc.astype(_BF16)
        f = acc[:, IN_TN - LANES:] + b_ref[...]
        log_f = jnp.minimum(f, 0.0) - jnp.log1p(jnp.exp(-jnp.abs(f)))
        hi, mid, lo = _split3_bf16(log_f)
        zero = jnp.zeros_like(hi)
        parts = jnp.where(lane < 8, hi, jnp.where(lane < 16, mid, jnp.where(lane < 24, lo, zero)))
        r = lax.broadcasted_iota(jnp.int32, (CUM_ROWS, CUM_ROWS), 0)
        c = lax.broadcasted_iota(jnp.int32, (CUM_ROWS, CUM_ROWS), 1)
        tri = jnp.where(c <= r, 1.0, 0.0).astype(_BF16)

        @pl.when(i % tiles_per_seq == 0)
        def _():
            carry_scr[...] = jnp.zeros_like(carry_scr)

        carry = carry_scr[...]
        chunks = []
        for ch in range(tm // CUM_ROWS):
            y = jnp.dot(tri, parts[ch * CUM_ROWS:(ch + 1) * CUM_ROWS], preferred_element_type=_F32)
            y = (y + pltpu.roll(y, 8, 1) + pltpu.roll(y, 16, 1)
                 + pltpu.roll(y, LANES - 8, 1) + pltpu.roll(y, LANES - 16, 1)) + carry
            carry = y[CUM_ROWS - 1:CUM_ROWS, :]
            chunks.append(y)
        carry_scr[...] = carry
        cum = jnp.concatenate(chunks, axis=0)
        dhi, dmid, dlo = _split3_bf16(cum * (-LOG2E))
        dec_ref[...] = jnp.where(lane < 8, dhi, jnp.where(lane < 16, dmid, jnp.where(lane < 24, dlo, zero)))

    def store_heads(acc):
        for h in range(FOX_HEADS):
            qkv_ref[0, h] = acc[:, h * LANES:(h + 1) * LANES].astype(_BF16)

    @pl.when(j == 1)
    def _():
        store_heads(jnp.dot(h_scr[...], wf_ref[...], preferred_element_type=_F32) * q_scale)
        ang = pos_ref[...] * invf_ref[...]
        sin = jnp.sin(ang)
        rot_ref[:, 0:LANES] = jnp.cos(ang)
        rot_ref[:, LANES:2 * LANES] = jnp.where((lane % QK_ROPE) < QK_ROPE // 2, -sin, sin)

    @pl.when(j > 1)
    def _():
        store_heads(jnp.dot(h_scr[...], wf_ref[...], preferred_element_type=_F32))


def _in_proj(x2d, g, w_misc, w_fox, b_row, pos_f, invf_row, seq):
    T = x2d.shape[0]
    tm = IN_TM
    grid = (T // tm, 1 + w_fox.shape[1] // IN_TN)
    kern = functools.partial(_in_proj_kernel, tiles_per_seq=seq // tm,
                             q_scale=LOG2E / math.sqrt(FOX_HEAD_DIM))
    return pl.pallas_call(
        kern,
        grid=grid,
        in_specs=[
            pl.BlockSpec((tm, D_MODEL), lambda i, j: (i, 0)),
            pl.BlockSpec((1, D_MODEL), lambda i, j: (0, 0)),
            pl.BlockSpec((D_MODEL, IN_TN), lambda i, j: (0, 0)),
            pl.BlockSpec((D_MODEL, IN_TN), lambda i, j: (0, jnp.maximum(j - 1, 0))),
            pl.BlockSpec((1, LANES), lambda i, j: (0, 0)),
            pl.BlockSpec((tm, 1), lambda i, j: (i, 0)),
            pl.BlockSpec((1, LANES), lambda i, j: (0, 0)),
        ],
        out_specs=[
            pl.BlockSpec((tm, IN_TN), lambda i, j: (i, 0)),
            pl.BlockSpec((tm, LANES), lambda i, j: (i, 0)),
            pl.BlockSpec((tm, 2 * LANES), lambda i, j: (i, 0)),
            pl.BlockSpec((1, FOX_HEADS, tm, LANES), lambda i, j: (jnp.maximum(j - 1, 0), 0, i, 0)),
        ],
        out_shape=[
            jax.ShapeDtypeStruct((T, IN_TN), _BF16),
            jax.ShapeDtypeStruct((T, LANES), _BF16),
            jax.ShapeDtypeStruct((T, 2 * LANES), _F32),
            jax.ShapeDtypeStruct((3, FOX_HEADS, T, LANES), _BF16),
        ],
        scratch_shapes=[
            pltpu.VMEM((tm, D_MODEL), _BF16),
            pltpu.VMEM((1, LANES), _F32),
        ],
        compiler_params=pltpu.CompilerParams(
            dimension_semantics=("arbitrary", "arbitrary"),
            vmem_limit_bytes=52 * 1024 * 1024),
        name="in_proj",
    )(x2d, g, w_misc, w_fox, b_row, pos_f, invf_row)


def _mla_up_kernel(misc_ref, rot_ref, gq_ref, gkv_ref, wuq_ref, wukv_ref,
                   q_ref, kn_ref, kpe_ref, v_ref, *, q_scale):
    tm = misc_ref.shape[0]
    q_lat = misc_ref[:, 0:Q_LORA].astype(_F32)
    kv_lat = misc_ref[:, Q_LORA:Q_LORA + KV_LORA].astype(_F32)
    k_rope = misc_ref[:, Q_LORA + KV_LORA:Q_LORA + KV_LORA + LANES].astype(_F32)

    qn = (_rms_scale(q_lat) * gq_ref[...]).astype(_BF16)
    kvn = (_rms_scale(kv_lat) * gkv_ref[...]).astype(_BF16)
    q = jnp.dot(qn, wuq_ref[...], preferred_element_type=_F32)
    kv = jnp.dot(kvn, wukv_ref[...], preferred_element_type=_F32)

    cos = rot_ref[:, 0:LANES]
    sin_signed = rot_ref[:, LANES:2 * LANES]
    lane = lax.broadcasted_iota(jnp.int32, (tm, LANES), 1)

    def rope(p):
        return p * cos + pltpu.roll(p, QK_ROPE // 2, 1) * sin_signed

    kpe_ref[...] = jnp.where(lane < QK_ROPE, rope(k_rope), 0.0).astype(_BF16)
    for h in range(MLA_HEADS):
        base = h * QK_WIDTH
        q_ref[h, :, 0:LANES] = (q[:, base:base + LANES] * q_scale).astype(_BF16)
        q_ref[h, :, LANES:QK_WIDTH] = (rope(q[:, base + LANES:base + QK_WIDTH]) * q_scale).astype(_BF16)
        kn_ref[h] = kv[:, base:base + LANES].astype(_BF16)
        v_ref[h] = kv[:, base + LANES:base + QK_WIDTH].astype(_BF16)


def _mla_up(misc, rot, gq, gkv, w_uq_r, w_ukv):
    T = misc.shape[0]
    tm = UP_TM
    kern = functools.partial(_mla_up_kernel, q_scale=LOG2E / math.sqrt(QK_NOPE + QK_ROPE))
    return pl.pallas_call(
        kern,
        grid=(T // tm,),
        in_specs=[
            pl.BlockSpec((tm, IN_TN), lambda i: (i, 0)),
            pl.BlockSpec((tm, 2 * LANES), lambda i: (i, 0)),
            pl.BlockSpec((1, Q_LORA), lambda i: (0, 0)),
            pl.BlockSpec((1, KV_LORA), lambda i: (0, 0)),
            pl.BlockSpec((Q_LORA, MLA_HEADS * QK_WIDTH), lambda i: (0, 0)),
            pl.BlockSpec((KV_LORA, MLA_HEADS * QK_WIDTH), lambda i: (0, 0)),
        ],
        out_specs=[
            pl.BlockSpec((MLA_HEADS, tm, QK_WIDTH), lambda i: (0, i, 0)),
            pl.BlockSpec((MLA_HEADS, tm, LANES), lambda i: (0, i, 0)),
            pl.BlockSpec((tm, LANES), lambda i: (i, 0)),
            pl.BlockSpec((MLA_HEADS, tm, LANES), lambda i: (0, i, 0)),
        ],
        out_shape=[
            jax.ShapeDtypeStruct((MLA_HEADS, T, QK_WIDTH), _BF16),
            jax.ShapeDtypeStruct((MLA_HEADS, T, LANES), _BF16),
            jax.ShapeDtypeStruct((T, LANES), _BF16),
            jax.ShapeDtypeStruct((MLA_HEADS, T, LANES), _BF16),
        ],
        compiler_params=pltpu.CompilerParams(
            dimension_semantics=("arbitrary",),
            vmem_limit_bytes=40 * 1024 * 1024),
        name="mla_up",
    )(misc, rot, gq, gkv, w_uq_r, w_ukv)


def _attn_kernel(q_ref, k_ref, kx_ref, v_ref, o_ref, s0_scr, s1_scr, m_scr, acc_scr, *,
                 causal_unit, one_hot_q):
    seq = k_ref.shape[0]
    t = ATT_T
    nt = seq // t
    row = lax.broadcasted_iota(jnp.int32, (t, t), 0)
    col = lax.broadcasted_iota(jnp.int32, (t, t), 1)
    if causal_unit == 1:
        allowed = col <= row
    else:
        allowed = (col // causal_unit) <= (row // causal_unit)
    ones = jnp.ones((t, LANES), _BF16)
    if one_hot_q:
        h = pl.program_id(1)
        lane = lax.broadcasted_iota(jnp.int32, (t, LANES), 1)
        pick = (lane == h) | (lane == h + FOX_HEADS) | (lane == h + 2 * FOX_HEADS)
        qx = jnp.where(pick, 1.0, 0.0).astype(_BF16)

    def q_rows(qs):
        q = q_ref[qs:qs + t, :]
        return jnp.concatenate([q, qx], axis=1) if one_hot_q else q

    def scores(q, ks):
        k = jnp.concatenate([k_ref[ks:ks + t, :], kx_ref[ks:ks + t, :]], axis=1)
        return lax.dot_general(q, k, (((1,), (1,)), ((), ())), preferred_element_type=_F32)

    def update(s, ks):
        m = m_scr[...]
        m_new = jnp.maximum(m, jnp.max(s, axis=1, keepdims=True))
        alpha = jnp.exp2(m - m_new)
        m_scr[...] = m_new
        p = jnp.concatenate(
            [jnp.exp2(s[:, c * LANES:(c + 1) * LANES] - m_new) for c in range(t // LANES)], axis=1)
        v = jnp.concatenate([v_ref[ks:ks + t, :], ones], axis=1)
        pv = jnp.dot(p.astype(_BF16), v, preferred_element_type=_F32)
        for c in range(2):
            sl = slice(c * LANES, (c + 1) * LANES)
            acc_scr[:, sl] = alpha * acc_scr[:, sl] + pv[:, sl]

    s_scr = (s0_scr, s1_scr)

    for i in range(nt):
        qs = i * t
        q = q_rows(qs)
        m_scr[...] = jnp.full(m_scr.shape, MASK_VALUE, _F32)
        acc_scr[...] = jnp.zeros_like(acc_scr)
        s_scr[0][...] = scores(q, 0)
        for j in range(i + 1):
            cur = s_scr[j % 2]
            if j < i:
                s_scr[(j + 1) % 2][...] = scores(q, (j + 1) * t)
                update(cur[...], j * t)
            else:
                update(jnp.where(allowed, cur[...], MASK_VALUE), j * t)
        o_ref[qs:qs + t, :] = (acc_scr[:, 0:LANES] / acc_scr[:, LANES:2 * LANES]).astype(o_ref.dtype)


def _attention(q, q_spec, k, k_spec, kx, v, v_spec, batch, seq, heads, causal_unit, one_hot_q, name):
    kern = functools.partial(_attn_kernel, causal_unit=causal_unit, one_hot_q=one_hot_q)
    return pl.pallas_call(
        kern,
        grid=(batch, heads),
        in_specs=[
            q_spec,
            k_spec,
            pl.BlockSpec((seq, LANES), lambda b, h: (b, 0)),
            v_spec,
        ],
        out_specs=pl.BlockSpec((seq, LANES), lambda b, h: (b, h)),
        out_shape=jax.ShapeDtypeStruct((batch * seq, heads * LANES), _BF16),
        scratch_shapes=[
            pltpu.VMEM((ATT_T, ATT_T), _F32),
            pltpu.VMEM((ATT_T, ATT_T), _F32),
            pltpu.VMEM((ATT_T, LANES), _F32),
            pltpu.VMEM((ATT_T, 2 * LANES), _F32),
        ],
        compiler_params=pltpu.CompilerParams(
            dimension_semantics=("arbitrary", "arbitrary"),
            vmem_limit_bytes=40 * 1024 * 1024),
        name=name,
    )(q, k, kx, v)


def _out_proj_kernel(om_ref, of_ref, gm_ref, gf_ref, w_ref, x_ref, o_ref):
    om = (_rms_scale(om_ref[...].astype(_F32)) * gm_ref[...]).astype(_BF16)
    of = (_rms_scale(of_ref[...].astype(_F32)) * gf_ref[...]).astype(_BF16)
    y = jnp.dot(om, w_ref[0:HEAD_WIDTH, :], preferred_element_type=_F32)
    y = y + jnp.dot(of, w_ref[HEAD_WIDTH:2 * HEAD_WIDTH, :], preferred_element_type=_F32)
    o_ref[...] = x_ref[...] + y


def _out_proj(o_mla, o_fox, gm, gf, w_out, x2d):
    T = x2d.shape[0]
    tm = OUT_TM
    return pl.pallas_call(
        _out_proj_kernel,
        grid=(T // tm,),
        in_specs=[
            pl.BlockSpec((tm, HEAD_WIDTH), lambda i: (i, 0)),
            pl.BlockSpec((tm, HEAD_WIDTH), lambda i: (i, 0)),
            pl.BlockSpec((1, HEAD_WIDTH), lambda i: (0, 0)),
            pl.BlockSpec((1, HEAD_WIDTH), lambda i: (0, 0)),
            pl.BlockSpec((2 * HEAD_WIDTH, D_MODEL), lambda i: (0, 0)),
            pl.BlockSpec((tm, D_MODEL), lambda i: (i, 0)),
        ],
        out_specs=pl.BlockSpec((tm, D_MODEL), lambda i: (i, 0)),
        out_shape=jax.ShapeDtypeStruct((T, D_MODEL), _F32),
        compiler_params=pltpu.CompilerParams(
            dimension_semantics=("arbitrary",),
            vmem_limit_bytes=48 * 1024 * 1024),
        name="out_proj",
    )(o_mla, o_fox, gm, gf, w_out, x2d)


def _ffn_kernel(x_ref, g_ref, wg_ref, wu_ref, wd_ref, gfin_ref, o_ref, h_scr, acc_scr):
    j = pl.program_id(1)

    @pl.when(j == 0)
    def _():
        h_scr[...] = (_rms_scale(x_ref[...]) * g_ref[...]).astype(_BF16)
        acc_scr[...] = jnp.zeros_like(acc_scr)

    h = h_scr[...]
    gate = jnp.dot(h, wg_ref[...], preferred_element_type=_F32)
    up = jnp.dot(h, wu_ref[...], preferred_element_type=_F32)
    a = (gate * (1.0 / (1.0 + jnp.exp(-gate))) * up).astype(_BF16)
    acc_scr[...] += jnp.dot(a, wd_ref[...], preferred_element_type=_F32)

    @pl.when(j == pl.num_programs(1) - 1)
    def _():
        y = x_ref[...] + acc_scr[...]
        o_ref[...] = _rms_scale(y) * gfin_ref[...]


def _ffn(x1, g, w_gate, w_up, w_down, g_final):
    T = x1.shape[0]
    tm, th = FFN_TM, FFN_TH
    return pl.pallas_call(
        _ffn_kernel,
        grid=(T // tm, FFN_HIDDEN // th),
        in_specs=[
            pl.BlockSpec((tm, D_MODEL), lambda i, j: (i, 0)),
            pl.BlockSpec((1, D_MODEL), lambda i, j: (0, 0)),
            pl.BlockSpec((D_MODEL, th), lambda i, j: (0, j)),
            pl.BlockSpec((D_MODEL, th), lambda i, j: (0, j)),
            pl.BlockSpec((th, D_MODEL), lambda i, j: (j, 0)),
            pl.BlockSpec((1, D_MODEL), lambda i, j: (0, 0)),
        ],
        out_specs=pl.BlockSpec((tm, D_MODEL), lambda i, j: (i, 0)),
        out_shape=jax.ShapeDtypeStruct((T, D_MODEL), _F32),
        scratch_shapes=[
            pltpu.VMEM((tm, D_MODEL), _BF16),
            pltpu.VMEM((tm, D_MODEL), _F32),
        ],
        compiler_params=pltpu.CompilerParams(
            dimension_semantics=("arbitrary", "arbitrary"),
            vmem_limit_bytes=48 * 1024 * 1024),
        name="ffn",
    )(x1, g, w_gate, w_up, w_down, g_final)


def _prep_w_in(w):
    w = w.astype(_BF16)
    lat = Q_LORA + KV_LORA
    kr = w[:, lat:lat + QK_ROPE]
    fox0 = lat + QK_ROPE
    f = w[:, fox0 + 3 * HEAD_WIDTH:]
    pad = jnp.zeros((w.shape[0], LANES - 3 * FOX_HEADS), w.dtype)
    misc = jnp.concatenate([w[:, :lat], kr, kr, f, f, f, pad], axis=1)
    return misc, w[:, fox0:fox0 + 3 * HEAD_WIDTH]


def _prep_w_uq(w):
    w3 = w.reshape(Q_LORA, MLA_HEADS, QK_NOPE + QK_ROPE)
    w3 = jnp.concatenate([w3, w3[:, :, QK_NOPE:]], axis=2)
    return w3.reshape(Q_LORA, MLA_HEADS * QK_WIDTH).astype(_BF16)


def kernel(x, positions, g_attn_norm, w_in, b_forget, g_q_lat, w_uq, g_kv_lat, w_ukv, g_out_mla, g_out_fox, w_out, g_ffn_norm, w_gate, w_up, w_down, g_final_norm):
    B, S, D = x.shape
    T = B * S
    assert w_in.shape[0] == 1, "one layer (DEPTH == 1) is supported"
    inv_freq = ROPE_THETA ** (-jnp.arange(0, QK_ROPE, 2, dtype=_F32) / QK_ROPE)
    invf_row = jnp.tile(inv_freq, LANES // (QK_ROPE // 2)).reshape(1, LANES)
    pos_f = positions.astype(_F32).reshape(T, 1)
    x2d = x.reshape(T, D)

    b_row = jnp.concatenate(
        [b_forget[0]] * 3 + [jnp.zeros((LANES - 3 * FOX_HEADS,), _F32)]).reshape(1, LANES)
    w_misc, w_fox = _prep_w_in(w_in[0])
    misc, dec, rot, fox = _in_proj(x2d, g_attn_norm[0].reshape(1, D), w_misc, w_fox, b_row,
                                   pos_f, invf_row, S)
    mq, mkn, mkpe, mv = _mla_up(misc, rot, g_q_lat[0].reshape(1, Q_LORA),
                                g_kv_lat[0].reshape(1, KV_LORA), _prep_w_uq(w_uq[0]), w_ukv[0].astype(_BF16))

    def head_spec(width):
        return pl.BlockSpec((None, S, width), lambda b, h: (h, b, 0))

    def slab_spec(slab):
        return pl.BlockSpec((None, None, S, LANES), lambda b, h: (slab, h, b, 0))

    o_mla = _attention(mq, head_spec(QK_WIDTH), mkn, head_spec(LANES), mkpe, mv, head_spec(LANES),
                       B, S, MLA_HEADS, CHUNK, False, "mla_attn")
    o_fox = _attention(fox, slab_spec(0), fox, slab_spec(1), dec, fox, slab_spec(2),
                       B, S, FOX_HEADS, 1, True, "fox_attn")
    x1 = _out_proj(o_mla, o_fox, g_out_mla[0].reshape(1, HEAD_WIDTH),
                   g_out_fox[0].reshape(1, HEAD_WIDTH), w_out[0].astype(_BF16), x2d)
    out = _ffn(x1, g_ffn_norm[0].reshape(1, D), w_gate[0].astype(_BF16), w_up[0].astype(_BF16),
               w_down[0].astype(_BF16), g_final_norm.reshape(1, D))
    return out.reshape(B, S, D)
```

```python
import functools
import math

import jax
import jax.numpy as jnp
from jax import lax
from jax.experimental import pallas as pl
from jax.experimental.pallas import tpu as pltpu

D_MODEL = 2048
CHUNK = 64
EPS = 1e-6
ROPE_THETA = 10000.0
MLA_HEADS = 8
Q_LORA = 512
KV_LORA = 256
QK_NOPE = 128
QK_ROPE = 64
V_HEAD = 128
FOX_HEADS = 8
FOX_HEAD_DIM = 128
HEAD_WIDTH = 1024
FFN_HIDDEN = 5632

LANES = 128
QK_WIDTH = 256
LOG2E = math.log2(math.e)
MASK_VALUE = -1e30

IN_TM = 512
IN_TN = 1024
CUM_ROWS = 256
UP_TM = 512
ATT_T = 512
OUT_TM = 512
FFN_TM = 1024
FFN_TH = 512

_BF16 = jnp.bfloat16
_F32 = jnp.float32


def _rms_scale(x):
    return x * lax.rsqrt(jnp.mean(x * x, axis=-1, keepdims=True) + EPS)


def _split3_bf16(x):
    hi = x.astype(_BF16)
    r = x - hi.astype(_F32)
    mid = r.astype(_BF16)
    lo = (r - mid.astype(_F32)).astype(_BF16)
    return hi, mid, lo


def _in_proj_kernel(x_ref, g_ref, wm_ref, wf_ref, b_ref, pos_ref, invf_ref,
                    misc_ref, dec_ref, rot_ref, qkv_ref,
                    h_scr, carry_scr, *, tiles_per_seq, q_scale):
    i = pl.program_id(0)
    j = pl.program_id(1)
    tm = x_ref.shape[0]
    lane = lax.broadcasted_iota(jnp.int32, (tm, LANES), 1)

    @pl.when(j == 0)
    def _():
        h = (_rms_scale(x_ref[...]) * g_ref[...]).astype(_BF16)
        h_scr[...] = h
        acc = jnp.dot(h, wm_ref[...], preferred_element_type=_F32)
        misc_ref[...] = acc.astype(_BF16)
        f = acc[:, IN_TN - LANES:] + b_ref[...]
        log_f = jnp.minimum(f, 0.0) - jnp.log1p(jnp.exp(-jnp.abs(f)))
        hi, mid, lo = _split3_bf16(log_f)
        zero = jnp.zeros_like(hi)
        parts = jnp.where(lane < 8, hi, jnp.where(lane < 16, mid, jnp.where(lane < 24, lo, zero)))
        r = lax.broadcasted_iota(jnp.int32, (CUM_ROWS, CUM_ROWS), 0)
        c = lax.broadcasted_iota(jnp.int32, (CUM_ROWS, CUM_ROWS), 1)
        tri = jnp.where(c <= r, 1.0, 0.0).astype(_BF16)

        @pl.when(i % tiles_per_seq == 0)
        def _():
            carry_scr[...] = jnp.zeros_like(carry_scr)

        carry = carry_scr[...]
        chunks = []
        for ch in range(tm // CUM_ROWS):
            y = jnp.dot(tri, parts[ch * CUM_ROWS:(ch + 1) * CUM_ROWS], preferred_element_type=_F32)
            y = (y + pltpu.roll(y, 8, 1) + pltpu.roll(y, 16, 1)
                 + pltpu.roll(y, LANES - 8, 1) + pltpu.roll(y, LANES - 16, 1)) + carry
            carry = y[CUM_ROWS - 1:CUM_ROWS, :]
            chunks.append(y)
        carry_scr[...] = carry
        cum = jnp.concatenate(chunks, axis=0)
        dhi, dmid, dlo = _split3_bf16(cum * (-LOG2E))
        dec_ref[...] = jnp.where(lane < 8, dhi, jnp.where(lane < 16, dmid, jnp.where(lane < 24, dlo, zero)))

    def store_heads(acc):
        for h in range(FOX_HEADS):
            qkv_ref[0, h] = acc[:, h * LANES:(h + 1) * LANES].astype(_BF16)

    slab = _fox_slab(i, j)

    @pl.when((j > 0) & (slab == 0))
    def _():
        store_heads(jnp.dot(h_scr[...], wf_ref[...], preferred_element_type=_F32) * q_scale)
        ang = pos_ref[...] * invf_ref[...]
        sin = jnp.sin(ang)
        rot_ref[:, 0:LANES] = jnp.cos(ang)
        rot_ref[:, LANES:2 * LANES] = jnp.where((lane % QK_ROPE) < QK_ROPE // 2, -sin, sin)

    @pl.when((j > 0) & (slab > 0))
    def _():
        store_heads(jnp.dot(h_scr[...], wf_ref[...], preferred_element_type=_F32))


def _fox_slab(i, j):
    return jnp.where(i % 2 == 0, jnp.maximum(j - 1, 0), jnp.minimum(3 - j, 2))


def _in_proj(x2d, g, w_misc, w_fox, b_row, pos_f, invf_row, seq):
    T = x2d.shape[0]
    tm = IN_TM
    grid = (T // tm, 1 + w_fox.shape[1] // IN_TN)
    kern = functools.partial(_in_proj_kernel, tiles_per_seq=seq // tm,
                             q_scale=LOG2E / math.sqrt(FOX_HEAD_DIM))
    return pl.pallas_call(
        kern,
        grid=grid,
        in_specs=[
            pl.BlockSpec((tm, D_MODEL), lambda i, j: (i, 0)),
            pl.BlockSpec((1, D_MODEL), lambda i, j: (0, 0)),
            pl.BlockSpec((D_MODEL, IN_TN), lambda i, j: (0, 0)),
            pl.BlockSpec((D_MODEL, IN_TN), lambda i, j: (0, _fox_slab(i, j))),
            pl.BlockSpec((1, LANES), lambda i, j: (0, 0)),
            pl.BlockSpec((tm, 1), lambda i, j: (i, 0)),
            pl.BlockSpec((1, LANES), lambda i, j: (0, 0)),
        ],
        out_specs=[
            pl.BlockSpec((tm, IN_TN), lambda i, j: (i, 0)),
            pl.BlockSpec((tm, LANES), lambda i, j: (i, 0)),
            pl.BlockSpec((tm, 2 * LANES), lambda i, j: (i, 0)),
            pl.BlockSpec((1, FOX_HEADS, tm, LANES), lambda i, j: (_fox_slab(i, j), 0, i, 0)),
        ],
        out_shape=[
            jax.ShapeDtypeStruct((T, IN_TN), _BF16),
            jax.ShapeDtypeStruct((T, LANES), _BF16),
            jax.ShapeDtypeStruct((T, 2 * LANES), _F32),
            jax.ShapeDtypeStruct((3, FOX_HEADS, T, LANES), _BF16),
        ],
        scratch_shapes=[
            pltpu.VMEM((tm, D_MODEL), _BF16),
            pltpu.VMEM((1, LANES), _F32),
        ],
        compiler_params=pltpu.CompilerParams(
            dimension_semantics=("arbitrary", "arbitrary"),
            vmem_limit_bytes=52 * 1024 * 1024),
        name="in_proj",
    )(x2d, g, w_misc, w_fox, b_row, pos_f, invf_row)


def _mla_up_kernel(misc_ref, rot_ref, gq_ref, gkv_ref, wuq_ref, wukv_ref,
                   q_ref, kn_ref, kpe_ref, v_ref, *, q_scale):
    tm = misc_ref.shape[0]
    q_lat = misc_ref[:, 0:Q_LORA].astype(_F32)
    kv_lat = misc_ref[:, Q_LORA:Q_LORA + KV_LORA].astype(_F32)
    k_rope = misc_ref[:, Q_LORA + KV_LORA:Q_LORA + KV_LORA + LANES].astype(_F32)

    qn = (_rms_scale(q_lat) * gq_ref[...]).astype(_BF16)
    kvn = (_rms_scale(kv_lat) * gkv_ref[...]).astype(_BF16)
    q = jnp.dot(qn, wuq_ref[...], preferred_element_type=_F32)
    kv = jnp.dot(kvn, wukv_ref[...], preferred_element_type=_F32)

    cos = rot_ref[:, 0:LANES]
    sin_signed = rot_ref[:, LANES:2 * LANES]
    lane = lax.broadcasted_iota(jnp.int32, (tm, LANES), 1)

    def rope(p):
        return p * cos + pltpu.roll(p, QK_ROPE // 2, 1) * sin_signed

    kpe_ref[...] = jnp.where(lane < QK_ROPE, rope(k_rope), 0.0).astype(_BF16)
    for h in range(MLA_HEADS):
        base = h * QK_WIDTH
        q_ref[h, :, 0:LANES] = (q[:, base:base + LANES] * q_scale).astype(_BF16)
        q_ref[h, :, LANES:QK_WIDTH] = (rope(q[:, base + LANES:base + QK_WIDTH]) * q_scale).astype(_BF16)
        kn_ref[h] = kv[:, base:base + LANES].astype(_BF16)
        v_ref[h] = kv[:, base + LANES:base + QK_WIDTH].astype(_BF16)


def _mla_up(misc, rot, gq, gkv, w_uq_r, w_ukv):
    T = misc.shape[0]
    tm = UP_TM
    kern = functools.partial(_mla_up_kernel, q_scale=LOG2E / math.sqrt(QK_NOPE + QK_ROPE))
    return pl.pallas_call(
        kern,
        grid=(T // tm,),
        in_specs=[
            pl.BlockSpec((tm, IN_TN), lambda i: (i, 0)),
            pl.BlockSpec((tm, 2 * LANES), lambda i: (i, 0)),
            pl.BlockSpec((1, Q_LORA), lambda i: (0, 0)),
            pl.BlockSpec((1, KV_LORA), lambda i: (0, 0)),
            pl.BlockSpec((Q_LORA, MLA_HEADS * QK_WIDTH), lambda i: (0, 0)),
            pl.BlockSpec((KV_LORA, MLA_HEADS * QK_WIDTH), lambda i: (0, 0)),
        ],
        out_specs=[
            pl.BlockSpec((MLA_HEADS, tm, QK_WIDTH), lambda i: (0, i, 0)),
            pl.BlockSpec((MLA_HEADS, tm, LANES), lambda i: (0, i, 0)),
            pl.BlockSpec((tm, LANES), lambda i: (i, 0)),
            pl.BlockSpec((MLA_HEADS, tm, LANES), lambda i: (0, i, 0)),
        ],
        out_shape=[
            jax.ShapeDtypeStruct((MLA_HEADS, T, QK_WIDTH), _BF16),
            jax.ShapeDtypeStruct((MLA_HEADS, T, LANES), _BF16),
            jax.ShapeDtypeStruct((T, LANES), _BF16),
            jax.ShapeDtypeStruct((MLA_HEADS, T, LANES), _BF16),
        ],
        compiler_params=pltpu.CompilerParams(
            dimension_semantics=("arbitrary",),
            vmem_limit_bytes=40 * 1024 * 1024),
        name="mla_up",
    )(misc, rot, gq, gkv, w_uq_r, w_ukv)


def _attn_kernel(q_ref, k_ref, kx_ref, v_ref, o_ref, s0_scr, s1_scr, m_scr, acc_scr, *,
                 causal_unit, one_hot_q):
    seq = k_ref.shape[0]
    t = ATT_T
    nt = seq // t
    row = lax.broadcasted_iota(jnp.int32, (t, t), 0)
    col = lax.broadcasted_iota(jnp.int32, (t, t), 1)
    if causal_unit == 1:
        allowed = col <= row
    else:
        allowed = (col // causal_unit) <= (row // causal_unit)
    ones = jnp.ones((t, LANES), _BF16)
    if one_hot_q:
        h = pl.program_id(1)
        lane = lax.broadcasted_iota(jnp.int32, (t, LANES), 1)
        pick = (lane == h) | (lane == h + FOX_HEADS) | (lane == h + 2 * FOX_HEADS)
        qx = jnp.where(pick, 1.0, 0.0).astype(_BF16)

    def q_rows(qs):
        q = q_ref[qs:qs + t, :]
        return jnp.concatenate([q, qx], axis=1) if one_hot_q else q

    def scores(q, ks):
        k = jnp.concatenate([k_ref[ks:ks + t, :], kx_ref[ks:ks + t, :]], axis=1)
        return lax.dot_general(q, k, (((1,), (1,)), ((), ())), preferred_element_type=_F32)

    def update(s, ks):
        m = m_scr[...]
        m_new = jnp.maximum(m, jnp.max(s, axis=1, keepdims=True))
        alpha = jnp.exp2(m - m_new)
        m_scr[...] = m_new
        p = jnp.concatenate(
            [jnp.exp2(s[:, c * LANES:(c + 1) * LANES] - m_new) for c in range(t // LANES)], axis=1)
        v = jnp.concatenate([v_ref[ks:ks + t, :], ones], axis=1)
        pv = jnp.dot(p.astype(_BF16), v, preferred_element_type=_F32)
        for c in range(2):
            sl = slice(c * LANES, (c + 1) * LANES)
            acc_scr[:, sl] = alpha * acc_scr[:, sl] + pv[:, sl]

    s_scr = (s0_scr, s1_scr)

    for i in range(nt):
        qs = i * t
        q = q_rows(qs)
        m_scr[...] = jnp.full(m_scr.shape, MASK_VALUE, _F32)
        acc_scr[...] = jnp.zeros_like(acc_scr)
        s_scr[0][...] = scores(q, 0)
        for j in range(i + 1):
            cur = s_scr[j % 2]
            if j < i:
                s_scr[(j + 1) % 2][...] = scores(q, (j + 1) * t)
                update(cur[...], j * t)
            else:
                update(jnp.where(allowed, cur[...], MASK_VALUE), j * t)
        o_ref[qs:qs + t, :] = (acc_scr[:, 0:LANES] / acc_scr[:, LANES:2 * LANES]).astype(o_ref.dtype)


def _attention(q, q_spec, k, k_spec, kx, v, v_spec, batch, seq, heads, causal_unit, one_hot_q, name):
    kern = functools.partial(_attn_kernel, causal_unit=causal_unit, one_hot_q=one_hot_q)
    return pl.pallas_call(
        kern,
        grid=(batch, heads),
        in_specs=[
            q_spec,
            k_spec,
            pl.BlockSpec((seq, LANES), lambda b, h: (b, 0)),
            v_spec,
        ],
        out_specs=pl.BlockSpec((seq, LANES), lambda b, h: (b, h)),
        out_shape=jax.ShapeDtypeStruct((batch * seq, heads * LANES), _BF16),
        scratch_shapes=[
            pltpu.VMEM((ATT_T, ATT_T), _F32),
            pltpu.VMEM((ATT_T, ATT_T), _F32),
            pltpu.VMEM((ATT_T, LANES), _F32),
            pltpu.VMEM((ATT_T, 2 * LANES), _F32),
        ],
        compiler_params=pltpu.CompilerParams(
            dimension_semantics=("arbitrary", "arbitrary"),
            vmem_limit_bytes=40 * 1024 * 1024),
        name=name,
    )(q, k, kx, v)


def _out_proj_kernel(om_ref, of_ref, gm_ref, gf_ref, w_ref, x_ref, o_ref):
    om = (_rms_scale(om_ref[...].astype(_F32)) * gm_ref[...]).astype(_BF16)
    of = (_rms_scale(of_ref[...].astype(_F32)) * gf_ref[...]).astype(_BF16)
    y = jnp.dot(om, w_ref[0:HEAD_WIDTH, :], preferred_element_type=_F32)
    y = y + jnp.dot(of, w_ref[HEAD_WIDTH:2 * HEAD_WIDTH, :], preferred_element_type=_F32)
    o_ref[...] = x_ref[...] + y


def _out_proj(o_mla, o_fox, gm, gf, w_out, x2d):
    T = x2d.shape[0]
    tm = OUT_TM
    return pl.pallas_call(
        _out_proj_kernel,
        grid=(T // tm,),
        in_specs=[
            pl.BlockSpec((tm, HEAD_WIDTH), lambda i: (i, 0)),
            pl.BlockSpec((tm, HEAD_WIDTH), lambda i: (i, 0)),
            pl.BlockSpec((1, HEAD_WIDTH), lambda i: (0, 0)),
            pl.BlockSpec((1, HEAD_WIDTH), lambda i: (0, 0)),
            pl.BlockSpec((2 * HEAD_WIDTH, D_MODEL), lambda i: (0, 0)),
            pl.BlockSpec((tm, D_MODEL), lambda i: (i, 0)),
        ],
        out_specs=pl.BlockSpec((tm, D_MODEL), lambda i: (i, 0)),
        out_shape=jax.ShapeDtypeStruct((T, D_MODEL), _F32),
        compiler_params=pltpu.CompilerParams(
            dimension_semantics=("arbitrary",),
            vmem_limit_bytes=48 * 1024 * 1024),
        name="out_proj",
    )(o_mla, o_fox, gm, gf, w_out, x2d)


def _ffn_kernel(x_ref, g_ref, wg_ref, wu_ref, wd_ref, gfin_ref, o_ref, h_scr):
    j = pl.program_id(1)

    @pl.when(j == 0)
    def _():
        h_scr[...] = (_rms_scale(x_ref[...]) * g_ref[...]).astype(_BF16)
        o_ref[...] = jnp.zeros_like(o_ref)

    h = h_scr[...]
    gate = jnp.dot(h, wg_ref[...], preferred_element_type=_F32)
    up = jnp.dot(h, wu_ref[...], preferred_element_type=_F32)
    a = (gate * (1.0 / (1.0 + jnp.exp(-gate))) * up).astype(_BF16)
    o_ref[...] += jnp.dot(a, wd_ref[...], preferred_element_type=_F32)

    @pl.when(j == pl.num_programs(1) - 1)
    def _():
        y = x_ref[...] + o_ref[...]
        o_ref[...] = _rms_scale(y) * gfin_ref[...]


def _ffn(x1, g, w_gate, w_up, w_down, g_final):
    T = x1.shape[0]
    tm, th = FFN_TM, FFN_TH
    nh = FFN_HIDDEN // th

    def hid(i, j):
        return jnp.where(i % 2 == 0, j, nh - 1 - j)

    return pl.pallas_call(
        _ffn_kernel,
        grid=(T // tm, nh),
        in_specs=[
            pl.BlockSpec((tm, D_MODEL), lambda i, j: (i, 0)),
            pl.BlockSpec((1, D_MODEL), lambda i, j: (0, 0)),
            pl.BlockSpec((D_MODEL, th), lambda i, j: (0, hid(i, j))),
            pl.BlockSpec((D_MODEL, th), lambda i, j: (0, hid(i, j))),
            pl.BlockSpec((th, D_MODEL), lambda i, j: (hid(i, j), 0)),
            pl.BlockSpec((1, D_MODEL), lambda i, j: (0, 0)),
        ],
        out_specs=pl.BlockSpec((tm, D_MODEL), lambda i, j: (i, 0)),
        out_shape=jax.ShapeDtypeStruct((T, D_MODEL), _F32),
        scratch_shapes=[pltpu.VMEM((tm, D_MODEL), _BF16)],
        compiler_params=pltpu.CompilerParams(
            dimension_semantics=("arbitrary", "arbitrary"),
            vmem_limit_bytes=63 * 1024 * 1024),
        name="ffn",
    )(x1, g, w_gate, w_up, w_down, g_final)


def _prep_w_in(w):
    w = w.astype(_BF16)
    lat = Q_LORA + KV_LORA
    kr = w[:, lat:lat + QK_ROPE]
    fox0 = lat + QK_ROPE
    f = w[:, fox0 + 3 * HEAD_WIDTH:]
    pad = jnp.zeros((w.shape[0], LANES - 3 * FOX_HEADS), w.dtype)
    misc = jnp.concatenate([w[:, :lat], kr, kr, f, f, f, pad], axis=1)
    return misc, w[:, fox0:fox0 + 3 * HEAD_WIDTH]


def _prep_w_uq(w):
    w3 = w.reshape(Q_LORA, MLA_HEADS, QK_NOPE + QK_ROPE)
    w3 = jnp.concatenate([w3, w3[:, :, QK_NOPE:]], axis=2)
    return w3.reshape(Q_LORA, MLA_HEADS * QK_WIDTH).astype(_BF16)


def kernel(x, positions, g_attn_norm, w_in, b_forget, g_q_lat, w_uq, g_kv_lat, w_ukv, g_out_mla, g_out_fox, w_out, g_ffn_norm, w_gate, w_up, w_down, g_final_norm):
    B, S, D = x.shape
    T = B * S
    assert w_in.shape[0] == 1, "one layer (DEPTH == 1) is supported"
    inv_freq = ROPE_THETA ** (-jnp.arange(0, QK_ROPE, 2, dtype=_F32) / QK_ROPE)
    invf_row = jnp.tile(inv_freq, LANES // (QK_ROPE // 2)).reshape(1, LANES)
    pos_f = positions.astype(_F32).reshape(T, 1)
    x2d = x.reshape(T, D)

    b_row = jnp.concatenate(
        [b_forget[0]] * 3 + [jnp.zeros((LANES - 3 * FOX_HEADS,), _F32)]).reshape(1, LANES)
    w_misc, w_fox = _prep_w_in(w_in[0])
    misc, dec, rot, fox = _in_proj(x2d, g_attn_norm[0].reshape(1, D), w_misc, w_fox, b_row,
                                   pos_f, invf_row, S)
    mq, mkn, mkpe, mv = _mla_up(misc, rot, g_q_lat[0].reshape(1, Q_LORA),
                                g_kv_lat[0].reshape(1, KV_LORA), _prep_w_uq(w_uq[0]), w_ukv[0].astype(_BF16))

    def head_spec(width):
        return pl.BlockSpec((None, S, width), lambda b, h: (h, b, 0))

    def slab_spec(slab):
        return pl.BlockSpec((None, None, S, LANES), lambda b, h: (slab, h, b, 0))

    o_mla = _attention(mq, head_spec(QK_WIDTH), mkn, head_spec(LANES), mkpe, mv, head_spec(LANES),
                       B, S, MLA_HEADS, CHUNK, False, "mla_attn")
    o_fox = _attention(fox, slab_spec(0), fox, slab_spec(1), dec, fox, slab_spec(2),
                       B, S, FOX_HEADS, 1, True, "fox_attn")
    x1 = _out_proj(o_mla, o_fox, g_out_mla[0].reshape(1, HEAD_WIDTH),
                   g_out_fox[0].reshape(1, HEAD_WIDTH), w_out[0].astype(_BF16), x2d)
    out = _ffn(x1, g_ffn_norm[0].reshape(1, D), w_gate[0].astype(_BF16), w_up[0].astype(_BF16),
               w_down[0].astype(_BF16), g_final_norm.reshape(1, D))
    return out.reshape(B, S, D)
```

```python
import functools
import math

import jax
import jax.numpy as jnp
from jax import lax
from jax.experimental import pallas as pl
from jax.experimental.pallas import tpu as pltpu

D_MODEL = 2048
CHUNK = 64
EPS = 1e-6
ROPE_THETA = 10000.0
MLA_HEADS = 8
Q_LORA = 512
KV_LORA = 256
QK_NOPE = 128
QK_ROPE = 64
V_HEAD = 128
FOX_HEADS = 8
FOX_HEAD_DIM = 128
HEAD_WIDTH = 1024
FFN_HIDDEN = 5632

LANES = 128
QK_WIDTH = 256
LOG2E = math.log2(math.e)
MASK_VALUE = -1e30

IN_TM = 1024
IN_TN = 1024
CUM_ROWS = 256
UP_TM = 1024
ATT_T = 512
ATT_SCORE_BUFS = 3
OUT_TM = 512
FFN_TM = 1024
FFN_TH = 512

_BF16 = jnp.bfloat16
_F32 = jnp.float32


def _rms_scale(x):
    return x * lax.rsqrt(jnp.mean(x * x, axis=-1, keepdims=True) + EPS)


def _split3_bf16(x):
    hi = x.astype(_BF16)
    r = x - hi.astype(_F32)
    mid = r.astype(_BF16)
    lo = (r - mid.astype(_F32)).astype(_BF16)
    return hi, mid, lo


def _in_proj_kernel(x_ref, g_ref, wm_ref, wf_ref, b_ref, pos_ref, invf_ref,
                    misc_ref, dec_ref, rot_ref, qkv_ref,
                    h_scr, carry_scr, *, tiles_per_seq, q_scale):
    i = pl.program_id(0)
    j = pl.program_id(1)
    tm = x_ref.shape[0]
    lane = lax.broadcasted_iota(jnp.int32, (tm, LANES), 1)

    @pl.when(j == 0)
    def _():
        h = (_rms_scale(x_ref[...]) * g_ref[...]).astype(_BF16)
        h_scr[...] = h
        acc = jnp.dot(h, wm_ref[...], preferred_element_type=_F32)
        misc_ref[...] = acc.astype(_BF16)
        f = acc[:, IN_TN - LANES:] + b_ref[...]
        log_f = jnp.minimum(f, 0.0) - jnp.log1p(jnp.exp(-jnp.abs(f)))
        hi, mid, lo = _split3_bf16(log_f)
        zero = jnp.zeros_like(hi)
        parts = jnp.where(lane < 8, hi, jnp.where(lane < 16, mid, jnp.where(lane < 24, lo, zero)))
        r = lax.broadcasted_iota(jnp.int32, (CUM_ROWS, CUM_ROWS), 0)
        c = lax.broadcasted_iota(jnp.int32, (CUM_ROWS, CUM_ROWS), 1)
        tri = jnp.where(c <= r, 1.0, 0.0).astype(_BF16)

        @pl.when(i % tiles_per_seq == 0)
        def _():
            carry_scr[...] = jnp.zeros_like(carry_scr)

        carry = carry_scr[...]
        chunks = []
        for ch in range(tm // CUM_ROWS):
            y = jnp.dot(tri, parts[ch * CUM_ROWS:(ch + 1) * CUM_ROWS], preferred_element_type=_F32)
            y = (y + pltpu.roll(y, 8, 1) + pltpu.roll(y, 16, 1)
                 + pltpu.roll(y, LANES - 8, 1) + pltpu.roll(y, LANES - 16, 1)) + carry
            carry = y[CUM_ROWS - 1:CUM_ROWS, :]
            chunks.append(y)
        carry_scr[...] = carry
        cum = jnp.concatenate(chunks, axis=0)
        dhi, dmid, dlo = _split3_bf16(cum * (-LOG2E))
        dec_ref[...] = jnp.where(lane < 8, dhi, jnp.where(lane < 16, dmid, jnp.where(lane < 24, dlo, zero)))

    def store_heads(acc):
        for h in range(FOX_HEADS):
            qkv_ref[0, h] = acc[:, h * LANES:(h + 1) * LANES].astype(_BF16)

    slab = _fox_slab(i, j)

    @pl.when((j > 0) & (slab == 0))
    def _():
        store_heads(jnp.dot(h_scr[...], wf_ref[...], preferred_element_type=_F32) * q_scale)
        ang = pos_ref[...] * invf_ref[...]
        sin = jnp.sin(ang)
        rot_ref[:, 0:LANES] = jnp.cos(ang)
        rot_ref[:, LANES:2 * LANES] = jnp.where((lane % QK_ROPE) < QK_ROPE // 2, -sin, sin)

    @pl.when((j > 0) & (slab > 0))
    def _():
        store_heads(jnp.dot(h_scr[...], wf_ref[...], preferred_element_type=_F32))


def _fox_slab(i, j):
    return jnp.where(i % 2 == 0, jnp.maximum(j - 1, 0), jnp.minimum(3 - j, 2))


def _in_proj(x2d, g, w_misc, w_fox, b_row, pos_f, invf_row, seq):
    T = x2d.shape[0]
    tm = IN_TM
    grid = (T // tm, 1 + w_fox.shape[1] // IN_TN)
    kern = functools.partial(_in_proj_kernel, tiles_per_seq=seq // tm,
                             q_scale=LOG2E / math.sqrt(FOX_HEAD_DIM))
    return pl.pallas_call(
        kern,
        grid=grid,
        in_specs=[
            pl.BlockSpec((tm, D_MODEL), lambda i, j: (i, 0)),
            pl.BlockSpec((1, D_MODEL), lambda i, j: (0, 0)),
            pl.BlockSpec((D_MODEL, IN_TN), lambda i, j: (0, 0)),
            pl.BlockSpec((D_MODEL, IN_TN), lambda i, j: (0, _fox_slab(i, j))),
            pl.BlockSpec((1, LANES), lambda i, j: (0, 0)),
            pl.BlockSpec((tm, 1), lambda i, j: (i, 0)),
            pl.BlockSpec((1, LANES), lambda i, j: (0, 0)),
        ],
        out_specs=[
            pl.BlockSpec((tm, IN_TN), lambda i, j: (i, 0)),
            pl.BlockSpec((tm, LANES), lambda i, j: (i, 0)),
            pl.BlockSpec((tm, 2 * LANES), lambda i, j: (i, 0)),
            pl.BlockSpec((1, FOX_HEADS, tm, LANES), lambda i, j: (_fox_slab(i, j), 0, i, 0)),
        ],
        out_shape=[
            jax.ShapeDtypeStruct((T, IN_TN), _BF16),
            jax.ShapeDtypeStruct((T, LANES), _BF16),
            jax.ShapeDtypeStruct((T, 2 * LANES), _F32),
            jax.ShapeDtypeStruct((3, FOX_HEADS, T, LANES), _BF16),
        ],
        scratch_shapes=[
            pltpu.VMEM((tm, D_MODEL), _BF16),
            pltpu.VMEM((1, LANES), _F32),
        ],
        compiler_params=pltpu.CompilerParams(
            dimension_semantics=("arbitrary", "arbitrary"),
            vmem_limit_bytes=60 * 1024 * 1024),
        name="in_proj",
    )(x2d, g, w_misc, w_fox, b_row, pos_f, invf_row)


def _mla_up_kernel(misc_ref, rot_ref, gq_ref, gkv_ref, wuq_ref, wukv_ref,
                   q_ref, kn_ref, kpe_ref, v_ref, *, q_scale):
    tm = misc_ref.shape[0]
    q_lat = misc_ref[:, 0:Q_LORA].astype(_F32)
    kv_lat = misc_ref[:, Q_LORA:Q_LORA + KV_LORA].astype(_F32)
    k_rope = misc_ref[:, Q_LORA + KV_LORA:Q_LORA + KV_LORA + LANES].astype(_F32)

    qn = (_rms_scale(q_lat) * gq_ref[...]).astype(_BF16)
    kvn = (_rms_scale(kv_lat) * gkv_ref[...]).astype(_BF16)
    q = jnp.dot(qn, wuq_ref[...], preferred_element_type=_F32)
    kv = jnp.dot(kvn, wukv_ref[...], preferred_element_type=_F32)

    cos = rot_ref[:, 0:LANES]
    sin_signed = rot_ref[:, LANES:2 * LANES]
    lane = lax.broadcasted_iota(jnp.int32, (tm, LANES), 1)

    def rope(p):
        return p * cos + pltpu.roll(p, QK_ROPE // 2, 1) * sin_signed

    kpe_ref[...] = jnp.where(lane < QK_ROPE, rope(k_rope), 0.0).astype(_BF16)
    for h in range(MLA_HEADS):
        base = h * QK_WIDTH
        q_ref[h, :, 0:LANES] = (q[:, base:base + LANES] * q_scale).astype(_BF16)
        q_ref[h, :, LANES:QK_WIDTH] = (rope(q[:, base + LANES:base + QK_WIDTH]) * q_scale).astype(_BF16)
        kn_ref[h] = kv[:, base:base + LANES].astype(_BF16)
        v_ref[h] = kv[:, base + LANES:base + QK_WIDTH].astype(_BF16)


def _mla_up(misc, rot, gq, gkv, w_uq_r, w_ukv):
    T = misc.shape[0]
    tm = UP_TM
    kern = functools.partial(_mla_up_kernel, q_scale=LOG2E / math.sqrt(QK_NOPE + QK_ROPE))
    return pl.pallas_call(
        kern,
        grid=(T // tm,),
        in_specs=[
            pl.BlockSpec((tm, IN_TN), lambda i: (i, 0)),
            pl.BlockSpec((tm, 2 * LANES), lambda i: (i, 0)),
            pl.BlockSpec((1, Q_LORA), lambda i: (0, 0)),
            pl.BlockSpec((1, KV_LORA), lambda i: (0, 0)),
            pl.BlockSpec((Q_LORA, MLA_HEADS * QK_WIDTH), lambda i: (0, 0)),
            pl.BlockSpec((KV_LORA, MLA_HEADS * QK_WIDTH), lambda i: (0, 0)),
        ],
        out_specs=[
            pl.BlockSpec((MLA_HEADS, tm, QK_WIDTH), lambda i: (0, i, 0)),
            pl.BlockSpec((MLA_HEADS, tm, LANES), lambda i: (0, i, 0)),
            pl.BlockSpec((tm, LANES), lambda i: (i, 0)),
            pl.BlockSpec((MLA_HEADS, tm, LANES), lambda i: (0, i, 0)),
        ],
        out_shape=[
            jax.ShapeDtypeStruct((MLA_HEADS, T, QK_WIDTH), _BF16),
            jax.ShapeDtypeStruct((MLA_HEADS, T, LANES), _BF16),
            jax.ShapeDtypeStruct((T, LANES), _BF16),
            jax.ShapeDtypeStruct((MLA_HEADS, T, LANES), _BF16),
        ],
        compiler_params=pltpu.CompilerParams(
            dimension_semantics=("arbitrary",),
            vmem_limit_bytes=56 * 1024 * 1024),
        name="mla_up",
    )(misc, rot, gq, gkv, w_uq_r, w_ukv)


def _attn_kernel(q_ref, k_ref, kx_ref, v_ref, o_ref, *scratch, causal_unit, one_hot_q):
    s_scr, state_scr = scratch[:ATT_SCORE_BUFS], scratch[ATT_SCORE_BUFS:]
    seq = k_ref.shape[0]
    t = ATT_T
    nt = seq // t
    row = lax.broadcasted_iota(jnp.int32, (t, t), 0)
    col = lax.broadcasted_iota(jnp.int32, (t, t), 1)
    if causal_unit == 1:
        allowed = col <= row
    else:
        allowed = (col // causal_unit) <= (row // causal_unit)
    ones = jnp.ones((t, LANES), _BF16)
    if one_hot_q:
        h = pl.program_id(1)
        lane = lax.broadcasted_iota(jnp.int32, (t, LANES), 1)
        pick = (lane == h) | (lane == h + FOX_HEADS) | (lane == h + 2 * FOX_HEADS)
        qx = jnp.where(pick, 1.0, 0.0).astype(_BF16)

    def q_rows(qs):
        q = q_ref[qs:qs + t, :]
        return jnp.concatenate([q, qx], axis=1) if one_hot_q else q

    def scores(q, ks):
        k = jnp.concatenate([k_ref[ks:ks + t, :], kx_ref[ks:ks + t, :]], axis=1)
        return lax.dot_general(q, k, (((1,), (1,)), ((), ())), preferred_element_type=_F32)

    def update(s, ks, m_scr, acc_scr):
        m = m_scr[...]
        m_new = jnp.maximum(m, jnp.max(s, axis=1, keepdims=True))
        alpha = jnp.exp2(m - m_new)
        m_scr[...] = m_new
        p = jnp.concatenate(
            [jnp.exp2(s[:, c * LANES:(c + 1) * LANES] - m_new) for c in range(t // LANES)], axis=1)
        v = jnp.concatenate([v_ref[ks:ks + t, :], ones], axis=1)
        pv = jnp.dot(p.astype(_BF16), v, preferred_element_type=_F32)
        for c in range(2):
            sl = slice(c * LANES, (c + 1) * LANES)
            acc_scr[:, sl] = alpha * acc_scr[:, sl] + pv[:, sl]

    tiles = [(i, j) for i in range(nt) for j in range(i + 1)]

    def issue(n):
        i, j = tiles[n]
        s_scr[n % len(s_scr)][...] = scores(q_rows(i * t), j * t)

    issue(0)
    for n, (i, j) in enumerate(tiles):
        m_scr, acc_scr = state_scr[2 * (i % 2)], state_scr[2 * (i % 2) + 1]
        if j == 0:
            m_scr[...] = jnp.full(m_scr.shape, MASK_VALUE, _F32)
            acc_scr[...] = jnp.zeros_like(acc_scr)
        if n + 1 < len(tiles):
            issue(n + 1)
        s = s_scr[n % len(s_scr)][...]
        if j == i:
            s = jnp.where(allowed, s, MASK_VALUE)
        update(s, j * t, m_scr, acc_scr)
        if j == i:
            o_ref[i * t:(i + 1) * t, :] = (
                acc_scr[:, 0:LANES] / acc_scr[:, LANES:2 * LANES]).astype(o_ref.dtype)


def _attention(q, q_spec, k, k_spec, kx, v, v_spec, batch, seq, heads, causal_unit, one_hot_q, name):
    kern = functools.partial(_attn_kernel, causal_unit=causal_unit, one_hot_q=one_hot_q)
    return pl.pallas_call(
        kern,
        grid=(batch, heads),
        in_specs=[
            q_spec,
            k_spec,
            pl.BlockSpec((seq, LANES), lambda b, h: (b, 0)),
            v_spec,
        ],
        out_specs=pl.BlockSpec((seq, LANES), lambda b, h: (b, h)),
        out_shape=jax.ShapeDtypeStruct((batch * seq, heads * LANES), _BF16),
        scratch_shapes=(
            [pltpu.VMEM((ATT_T, ATT_T), _F32)] * ATT_SCORE_BUFS
            + [pltpu.VMEM((ATT_T, LANES), _F32), pltpu.VMEM((ATT_T, 2 * LANES), _F32)] * 2),
        compiler_params=pltpu.CompilerParams(
            dimension_semantics=("arbitrary", "arbitrary"),
            vmem_limit_bytes=40 * 1024 * 1024),
        name=name,
    )(q, k, kx, v)


def _out_proj_kernel(om_ref, of_ref, gm_ref, gf_ref, w_ref, x_ref, o_ref):
    om = (_rms_scale(om_ref[...].astype(_F32)) * gm_ref[...]).astype(_BF16)
    of = (_rms_scale(of_ref[...].astype(_F32)) * gf_ref[...]).astype(_BF16)
    y = jnp.dot(om, w_ref[0:HEAD_WIDTH, :], preferred_element_type=_F32)
    y = y + jnp.dot(of, w_ref[HEAD_WIDTH:2 * HEAD_WIDTH, :], preferred_element_type=_F32)
    o_ref[...] = x_ref[...] + y


def _out_proj(o_mla, o_fox, gm, gf, w_out, x2d):
    T = x2d.shape[0]
    tm = OUT_TM
    return pl.pallas_call(
        _out_proj_kernel,
        grid=(T // tm,),
        in_specs=[
            pl.BlockSpec((tm, HEAD_WIDTH), lambda i: (i, 0)),
            pl.BlockSpec((tm, HEAD_WIDTH), lambda i: (i, 0)),
            pl.BlockSpec((1, HEAD_WIDTH), lambda i: (0, 0)),
            pl.BlockSpec((1, HEAD_WIDTH), lambda i: (0, 0)),
            pl.BlockSpec((2 * HEAD_WIDTH, D_MODEL), lambda i: (0, 0)),
            pl.BlockSpec((tm, D_MODEL), lambda i: (i, 0)),
        ],
        out_specs=pl.BlockSpec((tm, D_MODEL), lambda i: (i, 0)),
        out_shape=jax.ShapeDtypeStruct((T, D_MODEL), _F32),
        compiler_params=pltpu.CompilerParams(
            dimension_semantics=("arbitrary",),
            vmem_limit_bytes=48 * 1024 * 1024),
        name="out_proj",
    )(o_mla, o_fox, gm, gf, w_out, x2d)


def _ffn_kernel(x_ref, g_ref, wg_ref, wu_ref, wd_ref, gfin_ref, o_ref, h_scr):
    j = pl.program_id(1)

    @pl.when(j == 0)
    def _():
        h_scr[...] = (_rms_scale(x_ref[...]) * g_ref[...]).astype(_BF16)
        o_ref[...] = jnp.zeros_like(o_ref)

    h = h_scr[...]
    gate = jnp.dot(h, wg_ref[...], preferred_element_type=_F32)
    up = jnp.dot(h, wu_ref[...], preferred_element_type=_F32)
    a = (gate * (1.0 / (1.0 + jnp.exp(-gate))) * up).astype(_BF16)
    o_ref[...] += jnp.dot(a, wd_ref[...], preferred_element_type=_F32)

    @pl.when(j == pl.num_programs(1) - 1)
    def _():
        y = x_ref[...] + o_ref[...]
        o_ref[...] = _rms_scale(y) * gfin_ref[...]


def _ffn(x1, g, w_gate, w_up, w_down, g_final):
    T = x1.shape[0]
    tm, th = FFN_TM, FFN_TH
    nh = FFN_HIDDEN // th

    def hid(i, j):
        return jnp.where(i % 2 == 0, j, nh - 1 - j)

    return pl.pallas_call(
        _ffn_kernel,
        grid=(T // tm, nh),
        in_specs=[
            pl.BlockSpec((tm, D_MODEL), lambda i, j: (i, 0)),
            pl.BlockSpec((1, D_MODEL), lambda i, j: (0, 0)),
            pl.BlockSpec((D_MODEL, th), lambda i, j: (0, hid(i, j))),
            pl.BlockSpec((D_MODEL, th), lambda i, j: (0, hid(i, j))),
            pl.BlockSpec((th, D_MODEL), lambda i, j: (hid(i, j), 0)),
            pl.BlockSpec((1, D_MODEL), lambda i, j: (0, 0)),
        ],
        out_specs=pl.BlockSpec((tm, D_MODEL), lambda i, j: (i, 0)),
        out_shape=jax.ShapeDtypeStruct((T, D_MODEL), _F32),
        scratch_shapes=[pltpu.VMEM((tm, D_MODEL), _BF16)],
        compiler_params=pltpu.CompilerParams(
            dimension_semantics=("arbitrary", "arbitrary"),
            vmem_limit_bytes=63 * 1024 * 1024),
        name="ffn",
    )(x1, g, w_gate, w_up, w_down, g_final)


def _prep_w_in(w):
    w = w.astype(_BF16)
    lat = Q_LORA + KV_LORA
    kr = w[:, lat:lat + QK_ROPE]
    fox0 = lat + QK_ROPE
    f = w[:, fox0 + 3 * HEAD_WIDTH:]
    pad = jnp.zeros((w.shape[0], LANES - 3 * FOX_HEADS), w.dtype)
    misc = jnp.concatenate([w[:, :lat], kr, kr, f, f, f, pad], axis=1)
    return misc, w[:, fox0:fox0 + 3 * HEAD_WIDTH]


def _prep_w_uq(w):
    w3 = w.reshape(Q_LORA, MLA_HEADS, QK_NOPE + QK_ROPE)
    w3 = jnp.concatenate([w3, w3[:, :, QK_NOPE:]], axis=2)
    return w3.reshape(Q_LORA, MLA_HEADS * QK_WIDTH).astype(_BF16)


def kernel(x, positions, g_attn_norm, w_in, b_forget, g_q_lat, w_uq, g_kv_lat, w_ukv, g_out_mla, g_out_fox, w_out, g_ffn_norm, w_gate, w_up, w_down, g_final_norm):
    B, S, D = x.shape
    T = B * S
    assert w_in.shape[0] == 1, "one layer (DEPTH == 1) is supported"
    inv_freq = ROPE_THETA ** (-jnp.arange(0, QK_ROPE, 2, dtype=_F32) / QK_ROPE)
    invf_row = jnp.tile(inv_freq, LANES // (QK_ROPE // 2)).reshape(1, LANES)
    pos_f = positions.astype(_F32).reshape(T, 1)
    x2d = x.reshape(T, D)

    b_row = jnp.concatenate(
        [b_forget[0]] * 3 + [jnp.zeros((LANES - 3 * FOX_HEADS,), _F32)]).reshape(1, LANES)
    w_misc, w_fox = _prep_w_in(w_in[0])
    misc, dec, rot, fox = _in_proj(x2d, g_attn_norm[0].reshape(1, D), w_misc, w_fox, b_row,
                                   pos_f, invf_row, S)
    mq, mkn, mkpe, mv = _mla_up(misc, rot, g_q_lat[0].reshape(1, Q_LORA),
                                g_kv_lat[0].reshape(1, KV_LORA), _prep_w_uq(w_uq[0]), w_ukv[0].astype(_BF16))

    def head_spec(width):
        return pl.BlockSpec((None, S, width), lambda b, h: (h, b, 0))

    def slab_spec(slab):
        return pl.BlockSpec((None, None, S, LANES), lambda b, h: (slab, h, b, 0))

    o_mla = _attention(mq, head_spec(QK_WIDTH), mkn, head_spec(LANES), mkpe, mv, head_spec(LANES),
                       B, S, MLA_HEADS, CHUNK, False, "mla_attn")
    o_fox = _attention(fox, slab_spec(0), fox, slab_spec(1), dec, fox, slab_spec(2),
                       B, S, FOX_HEADS, 1, True, "fox_attn")
    x1 = _out_proj(o_mla, o_fox, g_out_mla[0].reshape(1, HEAD_WIDTH),
                   g_out_fox[0].reshape(1, HEAD_WIDTH), w_out[0].astype(_BF16), x2d)
    out = _ffn(x1, g_ffn_norm[0].reshape(1, D), w_gate[0].astype(_BF16), w_up[0].astype(_BF16),
               w_down[0].astype(_BF16), g_final_norm.reshape(1, D))
    return out.reshape(B, S, D)
```

```python
import functools
import math

import jax
import jax.numpy as jnp
from jax import lax
from jax.experimental import pallas as pl
from jax.experimental.pallas import tpu as pltpu

D_MODEL = 2048
CHUNK = 64
EPS = 1e-6
ROPE_THETA = 10000.0
MLA_HEADS = 8
Q_LORA = 512
KV_LORA = 256
QK_NOPE = 128
QK_ROPE = 64
V_HEAD = 128
FOX_HEADS = 8
FOX_HEAD_DIM = 128
HEAD_WIDTH = 1024
FFN_HIDDEN = 5632

LANES = 128
QK_WIDTH = 256
LOG2E = math.log2(math.e)
MASK_VALUE = -1e30

IN_TM = 1024
IN_TN = 1024
CUM_ROWS = 256
UP_TM = 1024
ATT_T = 512
ATT_SCORE_BUFS = 3
OUT_TM = 512
FFN_TM = 1024
FFN_TH = 512

_BF16 = jnp.bfloat16
_F32 = jnp.float32


def _rms_scale(x):
    return x * lax.rsqrt(jnp.mean(x * x, axis=-1, keepdims=True) + EPS)


def _sincos(x):
    k = jnp.floor(x * (2.0 / math.pi) + 0.5)
    r = ((x - k * 1.5703125) - k * 4.837512969970703125e-4) - k * 7.54978995489188216e-8
    r2 = r * r
    sin_r = r + r * r2 * (-1.6666654611e-1 + r2 * (8.3321608736e-3 + r2 * -1.9515295891e-4))
    cos_r = 1.0 - 0.5 * r2 + r2 * r2 * (4.166664568298827e-2 + r2 * (-1.388731625493765e-3
                                                                    + r2 * 2.443315711809948e-5))
    half_k = jnp.floor(k * 0.5)
    odd = k - 2.0 * half_k
    sign = 1.0 - 2.0 * (half_k - 2.0 * jnp.floor(half_k * 0.5))
    return sign * (sin_r + odd * (cos_r - sin_r)), sign * (cos_r - odd * (cos_r + sin_r))


def _split3_bf16(x):
    hi = x.astype(_BF16)
    r = x - hi.astype(_F32)
    mid = r.astype(_BF16)
    lo = (r - mid.astype(_F32)).astype(_BF16)
    return hi, mid, lo


def _in_proj_kernel(x_ref, g_ref, wm_ref, wf_ref, b_ref, pos_ref, invf_ref,
                    misc_ref, dec_ref, rot_ref, qkv_ref,
                    h_scr, carry_scr, *, tiles_per_seq, q_scale):
    i = pl.program_id(0)
    j = pl.program_id(1)
    tm = x_ref.shape[0]
    lane = lax.broadcasted_iota(jnp.int32, (tm, LANES), 1)

    @pl.when(j == 0)
    def _():
        h = (_rms_scale(x_ref[...]) * g_ref[...]).astype(_BF16)
        h_scr[...] = h
        acc = jnp.dot(h, wm_ref[...], preferred_element_type=_F32)
        misc_ref[...] = acc.astype(_BF16)
        f = acc[:, IN_TN - LANES:] + b_ref[...]
        log_f = jnp.minimum(f, 0.0) - jnp.log1p(jnp.exp(-jnp.abs(f)))
        hi, mid, lo = _split3_bf16(log_f)
        zero = jnp.zeros_like(hi)
        parts = jnp.where(lane < 8, hi, jnp.where(lane < 16, mid, jnp.where(lane < 24, lo, zero)))
        r = lax.broadcasted_iota(jnp.int32, (CUM_ROWS, CUM_ROWS), 0)
        c = lax.broadcasted_iota(jnp.int32, (CUM_ROWS, CUM_ROWS), 1)
        tri = jnp.where(c <= r, 1.0, 0.0).astype(_BF16)

        @pl.when(i % tiles_per_seq == 0)
        def _():
            carry_scr[...] = jnp.zeros_like(carry_scr)

        carry = carry_scr[...]
        chunks = []
        for ch in range(tm // CUM_ROWS):
            y = jnp.dot(tri, parts[ch * CUM_ROWS:(ch + 1) * CUM_ROWS], preferred_element_type=_F32)
            y = (y + pltpu.roll(y, 8, 1) + pltpu.roll(y, 16, 1)
                 + pltpu.roll(y, LANES - 8, 1) + pltpu.roll(y, LANES - 16, 1)) + carry
            carry = y[CUM_ROWS - 1:CUM_ROWS, :]
            chunks.append(y)
        carry_scr[...] = carry
        cum = jnp.concatenate(chunks, axis=0)
        dhi, dmid, dlo = _split3_bf16(cum * (-LOG2E))
        dec_ref[...] = jnp.where(lane < 8, dhi, jnp.where(lane < 16, dmid, jnp.where(lane < 24, dlo, zero)))

    def store_heads(acc):
        for h in range(FOX_HEADS):
            qkv_ref[0, h] = acc[:, h * LANES:(h + 1) * LANES].astype(_BF16)

    slab = _fox_slab(i, j)

    @pl.when((j > 0) & (slab == 0))
    def _():
        store_heads(jnp.dot(h_scr[...], wf_ref[...], preferred_element_type=_F32) * q_scale)
        sin, cos = _sincos(pos_ref[...] * invf_ref[...])
        rot_ref[:, 0:LANES] = cos
        rot_ref[:, LANES:2 * LANES] = jnp.where((lane % QK_ROPE) < QK_ROPE // 2, -sin, sin)

    @pl.when((j > 0) & (slab > 0))
    def _():
        store_heads(jnp.dot(h_scr[...], wf_ref[...], preferred_element_type=_F32))


def _fox_slab(i, j):
    return jnp.where(i % 2 == 0, jnp.maximum(j - 1, 0), jnp.minimum(3 - j, 2))


def _in_proj(x2d, g, w_misc, w_fox, b_row, pos_f, invf_row, seq):
    T = x2d.shape[0]
    tm = IN_TM
    grid = (T // tm, 1 + w_fox.shape[1] // IN_TN)
    kern = functools.partial(_in_proj_kernel, tiles_per_seq=seq // tm,
                             q_scale=LOG2E / math.sqrt(FOX_HEAD_DIM))
    return pl.pallas_call(
        kern,
        grid=grid,
        in_specs=[
            pl.BlockSpec((tm, D_MODEL), lambda i, j: (i, 0)),
            pl.BlockSpec((1, D_MODEL), lambda i, j: (0, 0)),
            pl.BlockSpec((D_MODEL, IN_TN), lambda i, j: (0, 0)),
            pl.BlockSpec((D_MODEL, IN_TN), lambda i, j: (0, _fox_slab(i, j))),
            pl.BlockSpec((1, LANES), lambda i, j: (0, 0)),
            pl.BlockSpec((tm, 1), lambda i, j: (i, 0)),
            pl.BlockSpec((1, LANES), lambda i, j: (0, 0)),
        ],
        out_specs=[
            pl.BlockSpec((tm, IN_TN), lambda i, j: (i, 0)),
            pl.BlockSpec((tm, LANES), lambda i, j: (i, 0)),
            pl.BlockSpec((tm, 2 * LANES), lambda i, j: (i, 0)),
            pl.BlockSpec((1, FOX_HEADS, tm, LANES), lambda i, j: (_fox_slab(i, j), 0, i, 0)),
        ],
        out_shape=[
            jax.ShapeDtypeStruct((T, IN_TN), _BF16),
            jax.ShapeDtypeStruct((T, LANES), _BF16),
            jax.ShapeDtypeStruct((T, 2 * LANES), _F32),
            jax.ShapeDtypeStruct((3, FOX_HEADS, T, LANES), _BF16),
        ],
        scratch_shapes=[
            pltpu.VMEM((tm, D_MODEL), _BF16),
            pltpu.VMEM((1, LANES), _F32),
        ],
        compiler_params=pltpu.CompilerParams(
            dimension_semantics=("arbitrary", "arbitrary"),
            vmem_limit_bytes=60 * 1024 * 1024),
        name="in_proj",
    )(x2d, g, w_misc, w_fox, b_row, pos_f, invf_row)


def _mla_up_kernel(misc_ref, rot_ref, gq_ref, gkv_ref, wuq_ref, wukv_ref,
                   q_ref, kn_ref, kpe_ref, v_ref, *, q_scale):
    tm = misc_ref.shape[0]
    q_lat = misc_ref[:, 0:Q_LORA].astype(_F32)
    kv_lat = misc_ref[:, Q_LORA:Q_LORA + KV_LORA].astype(_F32)
    k_rope = misc_ref[:, Q_LORA + KV_LORA:Q_LORA + KV_LORA + LANES].astype(_F32)

    qn = (_rms_scale(q_lat) * gq_ref[...]).astype(_BF16)
    kvn = (_rms_scale(kv_lat) * gkv_ref[...]).astype(_BF16)
    q = jnp.dot(qn, wuq_ref[...], preferred_element_type=_F32)
    kv = jnp.dot(kvn, wukv_ref[...], preferred_element_type=_F32)

    cos = rot_ref[:, 0:LANES]
    sin_signed = rot_ref[:, LANES:2 * LANES]
    lane = lax.broadcasted_iota(jnp.int32, (tm, LANES), 1)

    def rope(p):
        return p * cos + pltpu.roll(p, QK_ROPE // 2, 1) * sin_signed

    kpe_ref[...] = jnp.where(lane < QK_ROPE, rope(k_rope), 0.0).astype(_BF16)
    for h in range(MLA_HEADS):
        base = h * QK_WIDTH
        q_ref[h, :, 0:LANES] = (q[:, base:base + LANES] * q_scale).astype(_BF16)
        q_ref[h, :, LANES:QK_WIDTH] = (rope(q[:, base + LANES:base + QK_WIDTH]) * q_scale).astype(_BF16)
        kn_ref[h] = kv[:, base:base + LANES].astype(_BF16)
        v_ref[h] = kv[:, base + LANES:base + QK_WIDTH].astype(_BF16)


def _mla_up(misc, rot, gq, gkv, w_uq_r, w_ukv):
    T = misc.shape[0]
    tm = UP_TM
    kern = functools.partial(_mla_up_kernel, q_scale=LOG2E / math.sqrt(QK_NOPE + QK_ROPE))
    return pl.pallas_call(
        kern,
        grid=(T // tm,),
        in_specs=[
            pl.BlockSpec((tm, IN_TN), lambda i: (i, 0)),
            pl.BlockSpec((tm, 2 * LANES), lambda i: (i, 0)),
            pl.BlockSpec((1, Q_LORA), lambda i: (0, 0)),
            pl.BlockSpec((1, KV_LORA), lambda i: (0, 0)),
            pl.BlockSpec((Q_LORA, MLA_HEADS * QK_WIDTH), lambda i: (0, 0)),
            pl.BlockSpec((KV_LORA, MLA_HEADS * QK_WIDTH), lambda i: (0, 0)),
        ],
        out_specs=[
            pl.BlockSpec((MLA_HEADS, tm, QK_WIDTH), lambda i: (0, i, 0)),
            pl.BlockSpec((MLA_HEADS, tm, LANES), lambda i: (0, i, 0)),
            pl.BlockSpec((tm, LANES), lambda i: (i, 0)),
            pl.BlockSpec((MLA_HEADS, tm, LANES), lambda i: (0, i, 0)),
        ],
        out_shape=[
            jax.ShapeDtypeStruct((MLA_HEADS, T, QK_WIDTH), _BF16),
            jax.ShapeDtypeStruct((MLA_HEADS, T, LANES), _BF16),
            jax.ShapeDtypeStruct((T, LANES), _BF16),
            jax.ShapeDtypeStruct((MLA_HEADS, T, LANES), _BF16),
        ],
        compiler_params=pltpu.CompilerParams(
            dimension_semantics=("arbitrary",),
            vmem_limit_bytes=56 * 1024 * 1024),
        name="mla_up",
    )(misc, rot, gq, gkv, w_uq_r, w_ukv)


def _attn_kernel(q_ref, k_ref, kx_ref, v_ref, o_ref, *scratch, causal_unit, one_hot_q):
    s_scr, state_scr = scratch[:ATT_SCORE_BUFS], scratch[ATT_SCORE_BUFS:]
    seq = k_ref.shape[0]
    t = ATT_T
    nt = seq // t
    row = lax.broadcasted_iota(jnp.int32, (t, t), 0)
    col = lax.broadcasted_iota(jnp.int32, (t, t), 1)
    if causal_unit == 1:
        allowed = col <= row
    else:
        allowed = (col // causal_unit) <= (row // causal_unit)
    ones = jnp.ones((t, LANES), _BF16)
    if one_hot_q:
        h = pl.program_id(1)
        lane = lax.broadcasted_iota(jnp.int32, (t, LANES), 1)
        pick = (lane == h) | (lane == h + FOX_HEADS) | (lane == h + 2 * FOX_HEADS)
        qx = jnp.where(pick, 1.0, 0.0).astype(_BF16)

    def q_rows(qs):
        q = q_ref[qs:qs + t, :]
        return jnp.concatenate([q, qx], axis=1) if one_hot_q else q

    def scores(q, ks, kw):
        k = jnp.concatenate([k_ref[ks:ks + kw, :], kx_ref[ks:ks + kw, :]], axis=1)
        return lax.dot_general(q, k, (((1,), (1,)), ((), ())), preferred_element_type=_F32)

    def update(s, rows, ks, m_scr, acc_scr):
        kw = s.shape[1]
        m = m_scr[rows, :]
        m_new = jnp.maximum(m, jnp.max(s, axis=1, keepdims=True))
        alpha = jnp.exp2(m - m_new)
        m_scr[rows, :] = m_new
        p = jnp.concatenate(
            [jnp.exp2(s[:, c * LANES:(c + 1) * LANES] - m_new) for c in range(kw // LANES)], axis=1)
        v = jnp.concatenate([v_ref[ks:ks + kw, :], ones[0:kw]], axis=1)
        pv = jnp.dot(p.astype(_BF16), v, preferred_element_type=_F32)
        for c in range(2):
            sl = slice(c * LANES, (c + 1) * LANES)
            acc_scr[rows, sl] = alpha * acc_scr[rows, sl] + pv[:, sl]

    tiles = [(i, j) for i in range(nt) for j in range(i + 1)]
    half = t // 2
    top, bottom, full = slice(0, half), slice(half, t), slice(0, t)

    def issue(n):
        i, j = tiles[n]
        buf = s_scr[n % len(s_scr)]
        q = q_rows(i * t)
        if j == i:
            buf[top, 0:half] = scores(q[top], j * t, half)
            buf[bottom, :] = scores(q[bottom], j * t, t)
        else:
            buf[...] = scores(q, j * t, t)

    issue(0)
    for n, (i, j) in enumerate(tiles):
        m_scr, acc_scr = state_scr[2 * (i % 2)], state_scr[2 * (i % 2) + 1]
        if j == 0:
            m_scr[...] = jnp.full(m_scr.shape, MASK_VALUE, _F32)
            acc_scr[...] = jnp.zeros_like(acc_scr)
        if n + 1 < len(tiles):
            issue(n + 1)
        buf = s_scr[n % len(s_scr)]
        if j == i:
            update(jnp.where(allowed[top, 0:half], buf[top, 0:half], MASK_VALUE), top, j * t, m_scr, acc_scr)
            update(jnp.where(allowed[bottom, :], buf[bottom, :], MASK_VALUE), bottom, j * t, m_scr, acc_scr)
        else:
            update(buf[...], full, j * t, m_scr, acc_scr)
        if j == i:
            o_ref[i * t:(i + 1) * t, :] = (
                acc_scr[:, 0:LANES] / acc_scr[:, LANES:2 * LANES]).astype(o_ref.dtype)


def _attention(q, q_spec, k, k_spec, kx, v, v_spec, batch, seq, heads, causal_unit, one_hot_q, name):
    kern = functools.partial(_attn_kernel, causal_unit=causal_unit, one_hot_q=one_hot_q)
    return pl.pallas_call(
        kern,
        grid=(batch, heads),
        in_specs=[
            q_spec,
            k_spec,
            pl.BlockSpec((seq, LANES), lambda b, h: (b, 0)),
            v_spec,
        ],
        out_specs=pl.BlockSpec((seq, LANES), lambda b, h: (b, h)),
        out_shape=jax.ShapeDtypeStruct((batch * seq, heads * LANES), _BF16),
        scratch_shapes=(
            [pltpu.VMEM((ATT_T, ATT_T), _F32)] * ATT_SCORE_BUFS
            + [pltpu.VMEM((ATT_T, LANES), _F32), pltpu.VMEM((ATT_T, 2 * LANES), _F32)] * 2),
        compiler_params=pltpu.CompilerParams(
            dimension_semantics=("arbitrary", "arbitrary"),
            vmem_limit_bytes=40 * 1024 * 1024),
        name=name,
    )(q, k, kx, v)


def _out_proj_kernel(om_ref, of_ref, gm_ref, gf_ref, w_ref, x_ref, o_ref):
    om = (_rms_scale(om_ref[...].astype(_F32)) * gm_ref[...]).astype(_BF16)
    of = (_rms_scale(of_ref[...].astype(_F32)) * gf_ref[...]).astype(_BF16)
    y = jnp.dot(om, w_ref[0:HEAD_WIDTH, :], preferred_element_type=_F32)
    y = y + jnp.dot(of, w_ref[HEAD_WIDTH:2 * HEAD_WIDTH, :], preferred_element_type=_F32)
    o_ref[...] = x_ref[...] + y


def _out_proj(o_mla, o_fox, gm, gf, w_out, x2d):
    T = x2d.shape[0]
    tm = OUT_TM
    return pl.pallas_call(
        _out_proj_kernel,
        grid=(T // tm,),
        in_specs=[
            pl.BlockSpec((tm, HEAD_WIDTH), lambda i: (i, 0)),
            pl.BlockSpec((tm, HEAD_WIDTH), lambda i: (i, 0)),
            pl.BlockSpec((1, HEAD_WIDTH), lambda i: (0, 0)),
            pl.BlockSpec((1, HEAD_WIDTH), lambda i: (0, 0)),
            pl.BlockSpec((2 * HEAD_WIDTH, D_MODEL), lambda i: (0, 0)),
            pl.BlockSpec((tm, D_MODEL), lambda i: (i, 0)),
        ],
        out_specs=pl.BlockSpec((tm, D_MODEL), lambda i: (i, 0)),
        out_shape=jax.ShapeDtypeStruct((T, D_MODEL), _F32),
        compiler_params=pltpu.CompilerParams(
            dimension_semantics=("arbitrary",),
            vmem_limit_bytes=48 * 1024 * 1024),
        name="out_proj",
    )(o_mla, o_fox, gm, gf, w_out, x2d)


def _ffn_kernel(x_ref, g_ref, wg_ref, wu_ref, wd_ref, gfin_ref, o_ref, h_scr):
    j = pl.program_id(1)

    @pl.when(j == 0)
    def _():
        h_scr[...] = (_rms_scale(x_ref[...]) * g_ref[...]).astype(_BF16)
        o_ref[...] = jnp.zeros_like(o_ref)

    h = h_scr[...]
    gate = jnp.dot(h, wg_ref[...], preferred_element_type=_F32)
    up = jnp.dot(h, wu_ref[...], preferred_element_type=_F32)
    a = (gate * (1.0 / (1.0 + jnp.exp(-gate))) * up).astype(_BF16)
    o_ref[...] += jnp.dot(a, wd_ref[...], preferred_element_type=_F32)

    @pl.when(j == pl.num_programs(1) - 1)
    def _():
        y = x_ref[...] + o_ref[...]
        o_ref[...] = _rms_scale(y) * gfin_ref[...]


def _ffn(x1, g, w_gate, w_up, w_down, g_final):
    T = x1.shape[0]
    tm, th = FFN_TM, FFN_TH
    nh = FFN_HIDDEN // th

    def hid(i, j):
        return jnp.where(i % 2 == 0, j, nh - 1 - j)

    return pl.pallas_call(
        _ffn_kernel,
        grid=(T // tm, nh),
        in_specs=[
            pl.BlockSpec((tm, D_MODEL), lambda i, j: (i, 0)),
            pl.BlockSpec((1, D_MODEL), lambda i, j: (0, 0)),
            pl.BlockSpec((D_MODEL, th), lambda i, j: (0, hid(i, j))),
            pl.BlockSpec((D_MODEL, th), lambda i, j: (0, hid(i, j))),
            pl.BlockSpec((th, D_MODEL), lambda i, j: (hid(i, j), 0)),
            pl.BlockSpec((1, D_MODEL), lambda i, j: (0, 0)),
        ],
        out_specs=pl.BlockSpec((tm, D_MODEL), lambda i, j: (i, 0)),
        out_shape=jax.ShapeDtypeStruct((T, D_MODEL), _F32),
        scratch_shapes=[pltpu.VMEM((tm, D_MODEL), _BF16)],
        compiler_params=pltpu.CompilerParams(
            dimension_semantics=("arbitrary", "arbitrary"),
            vmem_limit_bytes=63 * 1024 * 1024),
        name="ffn",
    )(x1, g, w_gate, w_up, w_down, g_final)


def _prep_w_in(w):
    w = w.astype(_BF16)
    lat = Q_LORA + KV_LORA
    kr = w[:, lat:lat + QK_ROPE]
    fox0 = lat + QK_ROPE
    f = w[:, fox0 + 3 * HEAD_WIDTH:]
    pad = jnp.zeros((w.shape[0], LANES - 3 * FOX_HEADS), w.dtype)
    misc = jnp.concatenate([w[:, :lat], kr, kr, f, f, f, pad], axis=1)
    return misc, w[:, fox0:fox0 + 3 * HEAD_WIDTH]


def _prep_w_uq(w):
    w3 = w.reshape(Q_LORA, MLA_HEADS, QK_NOPE + QK_ROPE)
    w3 = jnp.concatenate([w3, w3[:, :, QK_NOPE:]], axis=2)
    return w3.reshape(Q_LORA, MLA_HEADS * QK_WIDTH).astype(_BF16)


def kernel(x, positions, g_attn_norm, w_in, b_forget, g_q_lat, w_uq, g_kv_lat, w_ukv, g_out_mla, g_out_fox, w_out, g_ffn_norm, w_gate, w_up, w_down, g_final_norm):
    B, S, D = x.shape
    T = B * S
    assert w_in.shape[0] == 1, "one layer (DEPTH == 1) is supported"
    inv_freq = ROPE_THETA ** (-jnp.arange(0, QK_ROPE, 2, dtype=_F32) / QK_ROPE)
    invf_row = jnp.tile(inv_freq, LANES // (QK_ROPE // 2)).reshape(1, LANES)
    pos_f = positions.astype(_F32).reshape(T, 1)
    x2d = x.reshape(T, D)

    b_row = jnp.concatenate(
        [b_forget[0]] * 3 + [jnp.zeros((LANES - 3 * FOX_HEADS,), _F32)]).reshape(1, LANES)
    w_misc, w_fox = _prep_w_in(w_in[0])
    misc, dec, rot, fox = _in_proj(x2d, g_attn_norm[0].reshape(1, D), w_misc, w_fox, b_row,
                                   pos_f, invf_row, S)
    mq, mkn, mkpe, mv = _mla_up(misc, rot, g_q_lat[0].reshape(1, Q_LORA),
                                g_kv_lat[0].reshape(1, KV_LORA), _prep_w_uq(w_uq[0]), w_ukv[0].astype(_BF16))

    def head_spec(width):
        return pl.BlockSpec((None, S, width), lambda b, h: (h, b, 0))

    def slab_spec(slab):
        return pl.BlockSpec((None, None, S, LANES), lambda b, h: (slab, h, b, 0))

    o_mla = _attention(mq, head_spec(QK_WIDTH), mkn, head_spec(LANES), mkpe, mv, head_spec(LANES),
                       B, S, MLA_HEADS, CHUNK, False, "mla_attn")
    o_fox = _attention(fox, slab_spec(0), fox, slab_spec(1), dec, fox, slab_spec(2),
                       B, S, FOX_HEADS, 1, True, "fox_attn")
    x1 = _out_proj(o_mla, o_fox, g_out_mla[0].reshape(1, HEAD_WIDTH),
                   g_out_fox[0].reshape(1, HEAD_WIDTH), w_out[0].astype(_BF16), x2d)
    out = _ffn(x1, g_ffn_norm[0].reshape(1, D), w_gate[0].astype(_BF16), w_up[0].astype(_BF16),
               w_down[0].astype(_BF16), g_final_norm.reshape(1, D))
    return out.reshape(B, S, D)
```

```python
import functools
import math

import jax
import jax.numpy as jnp
from jax import lax
from jax.experimental import pallas as pl
from jax.experimental.pallas import tpu as pltpu

D_MODEL = 2048
CHUNK = 64
EPS = 1e-6
ROPE_THETA = 10000.0
MLA_HEADS = 8
Q_LORA = 512
KV_LORA = 256
QK_NOPE = 128
QK_ROPE = 64
V_HEAD = 128
FOX_HEADS = 8
FOX_HEAD_DIM = 128
HEAD_WIDTH = 1024
FFN_HIDDEN = 5632

LANES = 128
QK_WIDTH = 256
LOG2E = math.log2(math.e)
MASK_VALUE = -1e30

IN_TM = 1024
IN_TN = 1024
CUM_ROWS = 256
UP_TM = 1024
ATT_T = 512
ATT_SCORE_BUFS = 3
OUT_TM = 512
FFN_TM = 1024
FFN_TH = 512

_BF16 = jnp.bfloat16
_F32 = jnp.float32


def _rms_scale(x):
    return x * lax.rsqrt(jnp.mean(x * x, axis=-1, keepdims=True) + EPS)


def _inv_rms(x):
    inv = lax.rsqrt(jnp.mean(x * x, axis=-1, keepdims=True) + EPS)
    return jnp.broadcast_to(inv, (x.shape[0], LANES))


def _scale_rows(y, inv):
    return jnp.concatenate(
        [y[:, c * LANES:(c + 1) * LANES] * inv for c in range(y.shape[1] // LANES)], axis=1)


def _sincos(x):
    k = jnp.floor(x * (2.0 / math.pi) + 0.5)
    r = ((x - k * 1.5703125) - k * 4.837512969970703125e-4) - k * 7.54978995489188216e-8
    r2 = r * r
    sin_r = r + r * r2 * (-1.6666654611e-1 + r2 * (8.3321608736e-3 + r2 * -1.9515295891e-4))
    cos_r = 1.0 - 0.5 * r2 + r2 * r2 * (4.166664568298827e-2 + r2 * (-1.388731625493765e-3
                                                                    + r2 * 2.443315711809948e-5))
    half_k = jnp.floor(k * 0.5)
    odd = k - 2.0 * half_k
    sign = 1.0 - 2.0 * (half_k - 2.0 * jnp.floor(half_k * 0.5))
    return sign * (sin_r + odd * (cos_r - sin_r)), sign * (cos_r - odd * (cos_r + sin_r))


def _split3_bf16(x):
    hi = x.astype(_BF16)
    r = x - hi.astype(_F32)
    mid = r.astype(_BF16)
    lo = (r - mid.astype(_F32)).astype(_BF16)
    return hi, mid, lo


def _in_proj_kernel(x_ref, g_ref, wm_ref, wf_ref, b_ref, pos_ref, invf_ref,
                    misc_ref, dec_ref, rot_ref, qkv_ref,
                    h_scr, inv_scr, carry_scr, *, tiles_per_seq, q_scale):
    i = pl.program_id(0)
    j = pl.program_id(1)
    tm = x_ref.shape[0]
    lane = lax.broadcasted_iota(jnp.int32, (tm, LANES), 1)

    @pl.when(j == 0)
    def _():
        x = x_ref[...]
        h = (x * g_ref[...]).astype(_BF16)
        h_scr[...] = h
        inv = _inv_rms(x)
        inv_scr[...] = inv
        acc = _scale_rows(jnp.dot(h, wm_ref[...], preferred_element_type=_F32), inv)
        misc_ref[...] = acc.astype(_BF16)
        f = acc[:, IN_TN - LANES:] + b_ref[...]
        log_f = jnp.minimum(f, 0.0) - jnp.log1p(jnp.exp(-jnp.abs(f)))
        hi, mid, lo = _split3_bf16(log_f)
        zero = jnp.zeros_like(hi)
        parts = jnp.where(lane < 8, hi, jnp.where(lane < 16, mid, jnp.where(lane < 24, lo, zero)))
        r = lax.broadcasted_iota(jnp.int32, (CUM_ROWS, CUM_ROWS), 0)
        c = lax.broadcasted_iota(jnp.int32, (CUM_ROWS, CUM_ROWS), 1)
        tri = jnp.where(c <= r, 1.0, 0.0).astype(_BF16)

        @pl.when(i % tiles_per_seq == 0)
        def _():
            carry_scr[...] = jnp.zeros_like(carry_scr)

        carry = carry_scr[...]
        chunks = []
        for ch in range(tm // CUM_ROWS):
            y = jnp.dot(tri, parts[ch * CUM_ROWS:(ch + 1) * CUM_ROWS], preferred_element_type=_F32)
            y = (y + pltpu.roll(y, 8, 1) + pltpu.roll(y, 16, 1)
                 + pltpu.roll(y, LANES - 8, 1) + pltpu.roll(y, LANES - 16, 1)) + carry
            carry = y[CUM_ROWS - 1:CUM_ROWS, :]
            chunks.append(y)
        carry_scr[...] = carry
        cum = jnp.concatenate(chunks, axis=0)
        dhi, dmid, dlo = _split3_bf16(cum * (-LOG2E))
        dec_ref[...] = jnp.where(lane < 8, dhi, jnp.where(lane < 16, dmid, jnp.where(lane < 24, dlo, zero)))

    def store_heads(scale):
        acc = jnp.dot(h_scr[...], wf_ref[...], preferred_element_type=_F32)
        for h in range(FOX_HEADS):
            qkv_ref[0, h] = (acc[:, h * LANES:(h + 1) * LANES] * scale).astype(_BF16)

    slab = _fox_slab(i, j)

    @pl.when((j > 0) & (slab == 0))
    def _():
        store_heads(inv_scr[...] * q_scale)
        sin, cos = _sincos(pos_ref[...] * invf_ref[...])
        rot_ref[:, 0:LANES] = cos
        rot_ref[:, LANES:2 * LANES] = jnp.where((lane % QK_ROPE) < QK_ROPE // 2, -sin, sin)

    @pl.when((j > 0) & (slab > 0))
    def _():
        store_heads(inv_scr[...])


def _fox_slab(i, j):
    return jnp.where(i % 2 == 0, jnp.maximum(j - 1, 0), jnp.minimum(3 - j, 2))


def _in_proj(x2d, g, w_misc, w_fox, b_row, pos_f, invf_row, seq):
    T = x2d.shape[0]
    tm = IN_TM
    grid = (T // tm, 1 + w_fox.shape[1] // IN_TN)
    kern = functools.partial(_in_proj_kernel, tiles_per_seq=seq // tm,
                             q_scale=LOG2E / math.sqrt(FOX_HEAD_DIM))
    return pl.pallas_call(
        kern,
        grid=grid,
        in_specs=[
            pl.BlockSpec((tm, D_MODEL), lambda i, j: (i, 0)),
            pl.BlockSpec((1, D_MODEL), lambda i, j: (0, 0)),
            pl.BlockSpec((D_MODEL, IN_TN), lambda i, j: (0, 0)),
            pl.BlockSpec((D_MODEL, IN_TN), lambda i, j: (0, _fox_slab(i, j))),
            pl.BlockSpec((1, LANES), lambda i, j: (0, 0)),
            pl.BlockSpec((tm, 1), lambda i, j: (i, 0)),
            pl.BlockSpec((1, LANES), lambda i, j: (0, 0)),
        ],
        out_specs=[
            pl.BlockSpec((tm, IN_TN), lambda i, j: (i, 0)),
            pl.BlockSpec((tm, LANES), lambda i, j: (i, 0)),
            pl.BlockSpec((tm, 2 * LANES), lambda i, j: (i, 0)),
            pl.BlockSpec((1, FOX_HEADS, tm, LANES), lambda i, j: (_fox_slab(i, j), 0, i, 0)),
        ],
        out_shape=[
            jax.ShapeDtypeStruct((T, IN_TN), _BF16),
            jax.ShapeDtypeStruct((T, LANES), _BF16),
            jax.ShapeDtypeStruct((T, 2 * LANES), _F32),
            jax.ShapeDtypeStruct((3, FOX_HEADS, T, LANES), _BF16),
        ],
        scratch_shapes=[
            pltpu.VMEM((tm, D_MODEL), _BF16),
            pltpu.VMEM((tm, LANES), _F32),
            pltpu.VMEM((1, LANES), _F32),
        ],
        compiler_params=pltpu.CompilerParams(
            dimension_semantics=("arbitrary", "arbitrary"),
            vmem_limit_bytes=60 * 1024 * 1024),
        name="in_proj",
    )(x2d, g, w_misc, w_fox, b_row, pos_f, invf_row)


def _mla_up_kernel(misc_ref, rot_ref, gq_ref, gkv_ref, wuq_ref, wukv_ref,
                   q_ref, kn_ref, kpe_ref, v_ref, *, q_scale):
    tm = misc_ref.shape[0]
    q_lat = misc_ref[:, 0:Q_LORA].astype(_F32)
    kv_lat = misc_ref[:, Q_LORA:Q_LORA + KV_LORA].astype(_F32)
    k_rope = misc_ref[:, Q_LORA + KV_LORA:Q_LORA + KV_LORA + LANES].astype(_F32)

    qn = (_rms_scale(q_lat) * gq_ref[...]).astype(_BF16)
    kvn = (_rms_scale(kv_lat) * gkv_ref[...]).astype(_BF16)
    q = jnp.dot(qn, wuq_ref[...], preferred_element_type=_F32)
    kv = jnp.dot(kvn, wukv_ref[...], preferred_element_type=_F32)

    cos = rot_ref[:, 0:LANES]
    sin_signed = rot_ref[:, LANES:2 * LANES]
    lane = lax.broadcasted_iota(jnp.int32, (tm, LANES), 1)

    def rope(p):
        return p * cos + pltpu.roll(p, QK_ROPE // 2, 1) * sin_signed

    kpe_ref[...] = jnp.where(lane < QK_ROPE, rope(k_rope), 0.0).astype(_BF16)
    for h in range(MLA_HEADS):
        base = h * QK_WIDTH
        q_ref[h, :, 0:LANES] = (q[:, base:base + LANES] * q_scale).astype(_BF16)
        q_ref[h, :, LANES:QK_WIDTH] = (rope(q[:, base + LANES:base + QK_WIDTH]) * q_scale).astype(_BF16)
        kn_ref[h] = kv[:, base:base + LANES].astype(_BF16)
        v_ref[h] = kv[:, base + LANES:base + QK_WIDTH].astype(_BF16)


def _mla_up(misc, rot, gq, gkv, w_uq_r, w_ukv):
    T = misc.shape[0]
    tm = UP_TM
    kern = functools.partial(_mla_up_kernel, q_scale=LOG2E / math.sqrt(QK_NOPE + QK_ROPE))
    return pl.pallas_call(
        kern,
        grid=(T // tm,),
        in_specs=[
            pl.BlockSpec((tm, IN_TN), lambda i: (i, 0)),
            pl.BlockSpec((tm, 2 * LANES), lambda i: (i, 0)),
            pl.BlockSpec((1, Q_LORA), lambda i: (0, 0)),
            pl.BlockSpec((1, KV_LORA), lambda i: (0, 0)),
            pl.BlockSpec((Q_LORA, MLA_HEADS * QK_WIDTH), lambda i: (0, 0)),
            pl.BlockSpec((KV_LORA, MLA_HEADS * QK_WIDTH), lambda i: (0, 0)),
        ],
        out_specs=[
            pl.BlockSpec((MLA_HEADS, tm, QK_WIDTH), lambda i: (0, i, 0)),
            pl.BlockSpec((MLA_HEADS, tm, LANES), lambda i: (0, i, 0)),
            pl.BlockSpec((tm, LANES), lambda i: (i, 0)),
            pl.BlockSpec((MLA_HEADS, tm, LANES), lambda i: (0, i, 0)),
        ],
        out_shape=[
            jax.ShapeDtypeStruct((MLA_HEADS, T, QK_WIDTH), _BF16),
            jax.ShapeDtypeStruct((MLA_HEADS, T, LANES), _BF16),
            jax.ShapeDtypeStruct((T, LANES), _BF16),
            jax.ShapeDtypeStruct((MLA_HEADS, T, LANES), _BF16),
        ],
        compiler_params=pltpu.CompilerParams(
            dimension_semantics=("arbitrary",),
            vmem_limit_bytes=56 * 1024 * 1024),
        name="mla_up",
    )(misc, rot, gq, gkv, w_uq_r, w_ukv)


def _attn_kernel(q_ref, k_ref, kx_ref, v_ref, o_ref, *scratch, causal_unit, one_hot_q):
    s_scr, state_scr = scratch[:ATT_SCORE_BUFS], scratch[ATT_SCORE_BUFS:]
    seq = k_ref.shape[0]
    t = ATT_T
    nt = seq // t
    row = lax.broadcasted_iota(jnp.int32, (t, t), 0)
    col = lax.broadcasted_iota(jnp.int32, (t, t), 1)
    if causal_unit == 1:
        allowed = col <= row
    else:
        allowed = (col // causal_unit) <= (row // causal_unit)
    ones = jnp.ones((t, LANES), _BF16)
    if one_hot_q:
        h = pl.program_id(1)
        lane = lax.broadcasted_iota(jnp.int32, (t, LANES), 1)
        pick = (lane == h) | (lane == h + FOX_HEADS) | (lane == h + 2 * FOX_HEADS)
        qx = jnp.where(pick, 1.0, 0.0).astype(_BF16)

    def q_rows(qs):
        q = q_ref[qs:qs + t, :]
        return jnp.concatenate([q, qx], axis=1) if one_hot_q else q

    def scores(q, ks, kw):
        k = jnp.concatenate([k_ref[ks:ks + kw, :], kx_ref[ks:ks + kw, :]], axis=1)
        return lax.dot_general(q, k, (((1,), (1,)), ((), ())), preferred_element_type=_F32)

    def update(s, rows, ks, m_scr, acc_scr):
        kw = s.shape[1]
        m = m_scr[rows, :]
        m_new = jnp.maximum(m, jnp.max(s, axis=1, keepdims=True))
        alpha = jnp.exp2(m - m_new)
        m_scr[rows, :] = m_new
        p = jnp.concatenate(
            [jnp.exp2(s[:, c * LANES:(c + 1) * LANES] - m_new) for c in range(kw // LANES)], axis=1)
        v = jnp.concatenate([v_ref[ks:ks + kw, :], ones[0:kw]], axis=1)
        pv = jnp.dot(p.astype(_BF16), v, preferred_element_type=_F32)
        for c in range(2):
            sl = slice(c * LANES, (c + 1) * LANES)
            acc_scr[rows, sl] = alpha * acc_scr[rows, sl] + pv[:, sl]

    tiles = [(i, j) for i in range(nt) for j in range(i + 1)]
    half = t // 2
    top, bottom, full = slice(0, half), slice(half, t), slice(0, t)

    def issue(n):
        i, j = tiles[n]
        buf = s_scr[n % len(s_scr)]
        q = q_rows(i * t)
        if j == i:
            buf[top, 0:half] = scores(q[top], j * t, half)
            buf[bottom, :] = scores(q[bottom], j * t, t)
        else:
            buf[...] = scores(q, j * t, t)

    issue(0)
    for n, (i, j) in enumerate(tiles):
        m_scr, acc_scr = state_scr[2 * (i % 2)], state_scr[2 * (i % 2) + 1]
        if j == 0:
            m_scr[...] = jnp.full(m_scr.shape, MASK_VALUE, _F32)
            acc_scr[...] = jnp.zeros_like(acc_scr)
        if n + 1 < len(tiles):
            issue(n + 1)
        buf = s_scr[n % len(s_scr)]
        if j == i:
            update(jnp.where(allowed[top, 0:half], buf[top, 0:half], MASK_VALUE), top, j * t, m_scr, acc_scr)
            update(jnp.where(allowed[bottom, :], buf[bottom, :], MASK_VALUE), bottom, j * t, m_scr, acc_scr)
        else:
            update(buf[...], full, j * t, m_scr, acc_scr)
        if j == i:
            o_ref[i * t:(i + 1) * t, :] = (
                acc_scr[:, 0:LANES] / acc_scr[:, LANES:2 * LANES]).astype(o_ref.dtype)


def _attention(q, q_spec, k, k_spec, kx, v, v_spec, batch, seq, heads, causal_unit, one_hot_q, name):
    kern = functools.partial(_attn_kernel, causal_unit=causal_unit, one_hot_q=one_hot_q)
    return pl.pallas_call(
        kern,
        grid=(batch, heads),
        in_specs=[
            q_spec,
            k_spec,
            pl.BlockSpec((seq, LANES), lambda b, h: (b, 0)),
            v_spec,
        ],
        out_specs=pl.BlockSpec((seq, LANES), lambda b, h: (b, h)),
        out_shape=jax.ShapeDtypeStruct((batch * seq, heads * LANES), _BF16),
        scratch_shapes=(
            [pltpu.VMEM((ATT_T, ATT_T), _F32)] * ATT_SCORE_BUFS
            + [pltpu.VMEM((ATT_T, LANES), _F32), pltpu.VMEM((ATT_T, 2 * LANES), _F32)] * 2),
        compiler_params=pltpu.CompilerParams(
            dimension_semantics=("arbitrary", "arbitrary"),
            vmem_limit_bytes=40 * 1024 * 1024),
        name=name,
    )(q, k, kx, v)


def _out_proj_kernel(om_ref, of_ref, gm_ref, gf_ref, w_ref, x_ref, o_ref):
    om = (_rms_scale(om_ref[...].astype(_F32)) * gm_ref[...]).astype(_BF16)
    of = (_rms_scale(of_ref[...].astype(_F32)) * gf_ref[...]).astype(_BF16)
    y = jnp.dot(om, w_ref[0:HEAD_WIDTH, :], preferred_element_type=_F32)
    y = y + jnp.dot(of, w_ref[HEAD_WIDTH:2 * HEAD_WIDTH, :], preferred_element_type=_F32)
    o_ref[...] = x_ref[...] + y


def _out_proj(o_mla, o_fox, gm, gf, w_out, x2d):
    T = x2d.shape[0]
    tm = OUT_TM
    return pl.pallas_call(
        _out_proj_kernel,
        grid=(T // tm,),
        in_specs=[
            pl.BlockSpec((tm, HEAD_WIDTH), lambda i: (i, 0)),
            pl.BlockSpec((tm, HEAD_WIDTH), lambda i: (i, 0)),
            pl.BlockSpec((1, HEAD_WIDTH), lambda i: (0, 0)),
            pl.BlockSpec((1, HEAD_WIDTH), lambda i: (0, 0)),
            pl.BlockSpec((2 * HEAD_WIDTH, D_MODEL), lambda i: (0, 0)),
            pl.BlockSpec((tm, D_MODEL), lambda i: (i, 0)),
        ],
        out_specs=pl.BlockSpec((tm, D_MODEL), lambda i: (i, 0)),
        out_shape=jax.ShapeDtypeStruct((T, D_MODEL), _F32),
        compiler_params=pltpu.CompilerParams(
            dimension_semantics=("arbitrary",),
            vmem_limit_bytes=48 * 1024 * 1024),
        name="out_proj",
    )(o_mla, o_fox, gm, gf, w_out, x2d)


def _ffn_kernel(x_ref, g_ref, wg_ref, wu_ref, wd_ref, gfin_ref, o_ref, h_scr, inv_scr):
    j = pl.program_id(1)

    def hidden_tile(h, inv):
        gate = _scale_rows(jnp.dot(h, wg_ref[...], preferred_element_type=_F32), inv)
        up = _scale_rows(jnp.dot(h, wu_ref[...], preferred_element_type=_F32), inv)
        a = (gate * (1.0 / (1.0 + jnp.exp(-gate))) * up).astype(_BF16)
        return jnp.dot(a, wd_ref[...], preferred_element_type=_F32)

    @pl.when(j == 0)
    def _():
        x = x_ref[...]
        h = (x * g_ref[...]).astype(_BF16)
        h_scr[...] = h
        inv = _inv_rms(x)
        inv_scr[...] = inv
        o_ref[...] = hidden_tile(h, inv)

    @pl.when(j > 0)
    def _():
        o_ref[...] += hidden_tile(h_scr[...], inv_scr[...])

    @pl.when(j == pl.num_programs(1) - 1)
    def _():
        y = x_ref[...] + o_ref[...]
        o_ref[...] = _rms_scale(y) * gfin_ref[...]


def _ffn(x1, g, w_gate, w_up, w_down, g_final):
    T = x1.shape[0]
    tm, th = FFN_TM, FFN_TH
    nh = FFN_HIDDEN // th

    def hid(i, j):
        return jnp.where(i % 2 == 0, j, nh - 1 - j)

    return pl.pallas_call(
        _ffn_kernel,
        grid=(T // tm, nh),
        in_specs=[
            pl.BlockSpec((tm, D_MODEL), lambda i, j: (i, 0)),
            pl.BlockSpec((1, D_MODEL), lambda i, j: (0, 0)),
            pl.BlockSpec((D_MODEL, th), lambda i, j: (0, hid(i, j))),
            pl.BlockSpec((D_MODEL, th), lambda i, j: (0, hid(i, j))),
            pl.BlockSpec((th, D_MODEL), lambda i, j: (hid(i, j), 0)),
            pl.BlockSpec((1, D_MODEL), lambda i, j: (0, 0)),
        ],
        out_specs=pl.BlockSpec((tm, D_MODEL), lambda i, j: (i, 0)),
        out_shape=jax.ShapeDtypeStruct((T, D_MODEL), _F32),
        scratch_shapes=[pltpu.VMEM((tm, D_MODEL), _BF16), pltpu.VMEM((tm, LANES), _F32)],
        compiler_params=pltpu.CompilerParams(
            dimension_semantics=("arbitrary", "arbitrary"),
            vmem_limit_bytes=63 * 1024 * 1024),
        name="ffn",
    )(x1, g, w_gate, w_up, w_down, g_final)


def _prep_w_in(w):
    w = w.astype(_BF16)
    lat = Q_LORA + KV_LORA
    kr = w[:, lat:lat + QK_ROPE]
    fox0 = lat + QK_ROPE
    f = w[:, fox0 + 3 * HEAD_WIDTH:]
    pad = jnp.zeros((w.shape[0], LANES - 3 * FOX_HEADS), w.dtype)
    misc = jnp.concatenate([w[:, :lat], kr, kr, f, f, f, pad], axis=1)
    return misc, w[:, fox0:fox0 + 3 * HEAD_WIDTH]


def _prep_w_uq(w):
    w3 = w.reshape(Q_LORA, MLA_HEADS, QK_NOPE + QK_ROPE)
    w3 = jnp.concatenate([w3, w3[:, :, QK_NOPE:]], axis=2)
    return w3.reshape(Q_LORA, MLA_HEADS * QK_WIDTH).astype(_BF16)


def kernel(x, positions, g_attn_norm, w_in, b_forget, g_q_lat, w_uq, g_kv_lat, w_ukv, g_out_mla, g_out_fox, w_out, g_ffn_norm, w_gate, w_up, w_down, g_final_norm):
    B, S, D = x.shape
    T = B * S
    assert w_in.shape[0] == 1, "one layer (DEPTH == 1) is supported"
    inv_freq = ROPE_THETA ** (-jnp.arange(0, QK_ROPE, 2, dtype=_F32) / QK_ROPE)
    invf_row = jnp.tile(inv_freq, LANES // (QK_ROPE // 2)).reshape(1, LANES)
    pos_f = positions.astype(_F32).reshape(T, 1)
    x2d = x.reshape(T, D)

    b_row = jnp.concatenate(
        [b_forget[0]] * 3 + [jnp.zeros((LANES - 3 * FOX_HEADS,), _F32)]).reshape(1, LANES)
    w_misc, w_fox = _prep_w_in(w_in[0])
    misc, dec, rot, fox = _in_proj(x2d, g_attn_norm[0].reshape(1, D), w_misc, w_fox, b_row,
                                   pos_f, invf_row, S)
    mq, mkn, mkpe, mv = _mla_up(misc, rot, g_q_lat[0].reshape(1, Q_LORA),
                                g_kv_lat[0].reshape(1, KV_LORA), _prep_w_uq(w_uq[0]), w_ukv[0].astype(_BF16))

    def head_spec(width):
        return pl.BlockSpec((None, S, width), lambda b, h: (h, b, 0))

    def slab_spec(slab):
        return pl.BlockSpec((None, None, S, LANES), lambda b, h: (slab, h, b, 0))

    o_mla = _attention(mq, head_spec(QK_WIDTH), mkn, head_spec(LANES), mkpe, mv, head_spec(LANES),
                       B, S, MLA_HEADS, CHUNK, False, "mla_attn")
    o_fox = _attention(fox, slab_spec(0), fox, slab_spec(1), dec, fox, slab_spec(2),
                       B, S, FOX_HEADS, 1, True, "fox_attn")
    x1 = _out_proj(o_mla, o_fox, g_out_mla[0].reshape(1, HEAD_WIDTH),
                   g_out_fox[0].reshape(1, HEAD_WIDTH), w_out[0].astype(_BF16), x2d)
    out = _ffn(x1, g_ffn_norm[0].reshape(1, D), w_gate[0].astype(_BF16), w_up[0].astype(_BF16),
               w_down[0].astype(_BF16), g_final_norm.reshape(1, D))
    return out.reshape(B, S, D)
```

```python
import functools
import math

import jax
import jax.numpy as jnp
from jax import lax
from jax.experimental import pallas as pl
from jax.experimental.pallas import tpu as pltpu

D_MODEL = 2048
CHUNK = 64
EPS = 1e-6
ROPE_THETA = 10000.0
MLA_HEADS = 8
Q_LORA = 512
KV_LORA = 256
QK_NOPE = 128
QK_ROPE = 64
V_HEAD = 128
FOX_HEADS = 8
FOX_HEAD_DIM = 128
HEAD_WIDTH = 1024
FFN_HIDDEN = 5632

LANES = 128
QK_WIDTH = 256
LOG2E = math.log2(math.e)
MASK_VALUE = -1e30

IN_TM = 1024
IN_TN = 1024
CUM_ROWS = 256
UP_TM = 1024
ATT_T = 512
ATT_SCORE_BUFS = 3
OUT_TM = 512
FFN_TM = 1024
FFN_TH = 512

_BF16 = jnp.bfloat16
_F32 = jnp.float32


def _rms_scale(x):
    return x * lax.rsqrt(jnp.mean(x * x, axis=-1, keepdims=True) + EPS)


def _inv_rms(x):
    inv = lax.rsqrt(jnp.mean(x * x, axis=-1, keepdims=True) + EPS)
    return jnp.broadcast_to(inv, (x.shape[0], LANES))


def _scale_rows(y, inv):
    return jnp.concatenate(
        [y[:, c * LANES:(c + 1) * LANES] * inv for c in range(y.shape[1] // LANES)], axis=1)


def _sincos(x):
    k = jnp.floor(x * (2.0 / math.pi) + 0.5)
    r = ((x - k * 1.5703125) - k * 4.837512969970703125e-4) - k * 7.54978995489188216e-8
    r2 = r * r
    sin_r = r + r * r2 * (-1.6666654611e-1 + r2 * (8.3321608736e-3 + r2 * -1.9515295891e-4))
    cos_r = 1.0 - 0.5 * r2 + r2 * r2 * (4.166664568298827e-2 + r2 * (-1.388731625493765e-3
                                                                    + r2 * 2.443315711809948e-5))
    half_k = jnp.floor(k * 0.5)
    odd = k - 2.0 * half_k
    sign = 1.0 - 2.0 * (half_k - 2.0 * jnp.floor(half_k * 0.5))
    return sign * (sin_r + odd * (cos_r - sin_r)), sign * (cos_r - odd * (cos_r + sin_r))


def _split3_bf16(x):
    hi = x.astype(_BF16)
    r = x - hi.astype(_F32)
    mid = r.astype(_BF16)
    lo = (r - mid.astype(_F32)).astype(_BF16)
    return hi, mid, lo


def _in_proj_kernel(x_ref, g_ref, wm_ref, wf_ref, b_ref, pos_ref, invf_ref,
                    misc_ref, dec_ref, rot_ref, qkv_ref,
                    h_scr, inv_scr, f_scr, carry_scr, *, tiles_per_seq, q_scale):
    i = pl.program_id(0)
    j = pl.program_id(1)
    tm = x_ref.shape[0]
    lane = lax.broadcasted_iota(jnp.int32, (tm, LANES), 1)

    @pl.when(j == 0)
    def _():
        x = x_ref[...]
        h = (x * g_ref[...]).astype(_BF16)
        h_scr[...] = h
        inv = _inv_rms(x)
        inv_scr[...] = inv
        acc = _scale_rows(jnp.dot(h, wm_ref[...], preferred_element_type=_F32), inv)
        misc_ref[...] = acc.astype(_BF16)
        f_scr[...] = acc[:, 0:LANES] + b_ref[...]

    def decay_columns():
        f = f_scr[...]
        log_f = jnp.minimum(f, 0.0) - jnp.log1p(jnp.exp(-jnp.abs(f)))
        hi, mid, lo = _split3_bf16(log_f)
        zero = jnp.zeros_like(hi)
        parts = jnp.where(lane < 8, hi, jnp.where(lane < 16, mid, jnp.where(lane < 24, lo, zero)))
        r = lax.broadcasted_iota(jnp.int32, (CUM_ROWS, CUM_ROWS), 0)
        c = lax.broadcasted_iota(jnp.int32, (CUM_ROWS, CUM_ROWS), 1)
        tri = jnp.where(c <= r, 1.0, 0.0).astype(_BF16)

        carry = jnp.where(i % tiles_per_seq == 0, 0.0, carry_scr[...])
        chunks = []
        for ch in range(tm // CUM_ROWS):
            y = jnp.dot(tri, parts[ch * CUM_ROWS:(ch + 1) * CUM_ROWS], preferred_element_type=_F32)
            y = (y + pltpu.roll(y, 8, 1) + pltpu.roll(y, 16, 1)
                 + pltpu.roll(y, LANES - 8, 1) + pltpu.roll(y, LANES - 16, 1))
            chunks.append(y + carry)
            carry = carry + y[CUM_ROWS - 1:CUM_ROWS, :]
        carry_scr[...] = carry
        cum = jnp.concatenate(chunks, axis=0)
        dhi, dmid, dlo = _split3_bf16(cum * (-LOG2E))
        dec_ref[...] = jnp.where(lane < 8, dhi, jnp.where(lane < 16, dmid, jnp.where(lane < 24, dlo, zero)))

    def store_heads(scale):
        acc = jnp.dot(h_scr[...], wf_ref[...], preferred_element_type=_F32)
        for h in range(FOX_HEADS):
            qkv_ref[0, h] = (acc[:, h * LANES:(h + 1) * LANES] * scale).astype(_BF16)

    slab = _fox_slab(i, j)

    @pl.when((j > 0) & (slab == 0))
    def _():
        store_heads(inv_scr[...] * q_scale)
        sin, cos = _sincos(pos_ref[...] * invf_ref[...])
        rot_ref[:, 0:LANES] = cos
        rot_ref[:, LANES:2 * LANES] = jnp.where((lane % QK_ROPE) < QK_ROPE // 2, -sin, sin)

    @pl.when((j > 0) & (slab == 1))
    def _():
        store_heads(inv_scr[...])
        decay_columns()

    @pl.when((j > 0) & (slab == 2))
    def _():
        store_heads(inv_scr[...])


def _fox_slab(i, j):
    return jnp.where(i % 2 == 0, jnp.maximum(j - 1, 0), jnp.minimum(3 - j, 2))


def _in_proj(x2d, g, w_misc, w_fox, b_row, pos_f, invf_row, seq):
    T = x2d.shape[0]
    tm = IN_TM
    grid = (T // tm, 1 + w_fox.shape[1] // IN_TN)
    kern = functools.partial(_in_proj_kernel, tiles_per_seq=seq // tm,
                             q_scale=LOG2E / math.sqrt(FOX_HEAD_DIM))
    return pl.pallas_call(
        kern,
        grid=grid,
        in_specs=[
            pl.BlockSpec((tm, D_MODEL), lambda i, j: (i, 0)),
            pl.BlockSpec((1, D_MODEL), lambda i, j: (0, 0)),
            pl.BlockSpec((D_MODEL, IN_TN), lambda i, j: (0, 0)),
            pl.BlockSpec((D_MODEL, IN_TN), lambda i, j: (0, _fox_slab(i, j))),
            pl.BlockSpec((1, LANES), lambda i, j: (0, 0)),
            pl.BlockSpec((tm, 1), lambda i, j: (i, 0)),
            pl.BlockSpec((1, LANES), lambda i, j: (0, 0)),
        ],
        out_specs=[
            pl.BlockSpec((tm, IN_TN), lambda i, j: (i, 0)),
            pl.BlockSpec((tm, LANES), lambda i, j: (i, 0)),
            pl.BlockSpec((tm, 2 * LANES), lambda i, j: (i, 0)),
            pl.BlockSpec((1, FOX_HEADS, tm, LANES), lambda i, j: (_fox_slab(i, j), 0, i, 0)),
        ],
        out_shape=[
            jax.ShapeDtypeStruct((T, IN_TN), _BF16),
            jax.ShapeDtypeStruct((T, LANES), _BF16),
            jax.ShapeDtypeStruct((T, 2 * LANES), _F32),
            jax.ShapeDtypeStruct((3, FOX_HEADS, T, LANES), _BF16),
        ],
        scratch_shapes=[
            pltpu.VMEM((tm, D_MODEL), _BF16),
            pltpu.VMEM((tm, LANES), _F32),
            pltpu.VMEM((tm, LANES), _F32),
            pltpu.VMEM((1, LANES), _F32),
        ],
        compiler_params=pltpu.CompilerParams(
            dimension_semantics=("arbitrary", "arbitrary"),
            vmem_limit_bytes=60 * 1024 * 1024),
        name="in_proj",
    )(x2d, g, w_misc, w_fox, b_row, pos_f, invf_row)


def _mla_up_kernel(misc_ref, rot_ref, gq_ref, gkv_ref, wuq_ref, wukv_ref,
                   q_ref, kn_ref, kpe_ref, v_ref, *, q_scale):
    tm = misc_ref.shape[0]
    q_lat = misc_ref[:, LANES:LANES + Q_LORA].astype(_F32)
    kv_lat = misc_ref[:, LANES + Q_LORA:LANES + Q_LORA + KV_LORA].astype(_F32)
    k_rope = misc_ref[:, LANES + Q_LORA + KV_LORA:IN_TN].astype(_F32)

    qn = (_rms_scale(q_lat) * gq_ref[...]).astype(_BF16)
    kvn = (_rms_scale(kv_lat) * gkv_ref[...]).astype(_BF16)
    q = jnp.dot(qn, wuq_ref[...], preferred_element_type=_F32)
    kv = jnp.dot(kvn, wukv_ref[...], preferred_element_type=_F32)

    cos = rot_ref[:, 0:LANES]
    sin_signed = rot_ref[:, LANES:2 * LANES]
    lane = lax.broadcasted_iota(jnp.int32, (tm, LANES), 1)

    def rope(p):
        return p * cos + pltpu.roll(p, QK_ROPE // 2, 1) * sin_signed

    kpe_ref[...] = jnp.where(lane < QK_ROPE, rope(k_rope), 0.0).astype(_BF16)
    for h in range(MLA_HEADS):
        base = h * QK_WIDTH
        q_ref[h, :, 0:LANES] = (q[:, base:base + LANES] * q_scale).astype(_BF16)
        q_ref[h, :, LANES:QK_WIDTH] = (rope(q[:, base + LANES:base + QK_WIDTH]) * q_scale).astype(_BF16)
        kn_ref[h] = kv[:, base:base + LANES].astype(_BF16)
        v_ref[h] = kv[:, base + LANES:base + QK_WIDTH].astype(_BF16)


def _mla_up(misc, rot, gq, gkv, w_uq_r, w_ukv):
    T = misc.shape[0]
    tm = UP_TM
    kern = functools.partial(_mla_up_kernel, q_scale=LOG2E / math.sqrt(QK_NOPE + QK_ROPE))
    return pl.pallas_call(
        kern,
        grid=(T // tm,),
        in_specs=[
            pl.BlockSpec((tm, IN_TN), lambda i: (i, 0)),
            pl.BlockSpec((tm, 2 * LANES), lambda i: (i, 0)),
            pl.BlockSpec((1, Q_LORA), lambda i: (0, 0)),
            pl.BlockSpec((1, KV_LORA), lambda i: (0, 0)),
            pl.BlockSpec((Q_LORA, MLA_HEADS * QK_WIDTH), lambda i: (0, 0)),
            pl.BlockSpec((KV_LORA, MLA_HEADS * QK_WIDTH), lambda i: (0, 0)),
        ],
        out_specs=[
            pl.BlockSpec((MLA_HEADS, tm, QK_WIDTH), lambda i: (0, i, 0)),
            pl.BlockSpec((MLA_HEADS, tm, LANES), lambda i: (0, i, 0)),
            pl.BlockSpec((tm, LANES), lambda i: (i, 0)),
            pl.BlockSpec((MLA_HEADS, tm, LANES), lambda i: (0, i, 0)),
        ],
        out_shape=[
            jax.ShapeDtypeStruct((MLA_HEADS, T, QK_WIDTH), _BF16),
            jax.ShapeDtypeStruct((MLA_HEADS, T, LANES), _BF16),
            jax.ShapeDtypeStruct((T, LANES), _BF16),
            jax.ShapeDtypeStruct((MLA_HEADS, T, LANES), _BF16),
        ],
        compiler_params=pltpu.CompilerParams(
            dimension_semantics=("arbitrary",),
            vmem_limit_bytes=56 * 1024 * 1024),
        name="mla_up",
    )(misc, rot, gq, gkv, w_uq_r, w_ukv)


def _attn_kernel(q_ref, k_ref, kx_ref, v_ref, o_ref, *scratch, causal_unit, one_hot_q):
    s_scr, state_scr = scratch[:ATT_SCORE_BUFS], scratch[ATT_SCORE_BUFS:]
    seq = k_ref.shape[0]
    t = ATT_T
    nt = seq // t
    row = lax.broadcasted_iota(jnp.int32, (t, t), 0)
    col = lax.broadcasted_iota(jnp.int32, (t, t), 1)
    if causal_unit == 1:
        allowed = col <= row
    else:
        allowed = (col // causal_unit) <= (row // causal_unit)
    ones = jnp.ones((t, LANES), _BF16)
    if one_hot_q:
        h = pl.program_id(1)
        lane = lax.broadcasted_iota(jnp.int32, (t, LANES), 1)
        pick = (lane == h) | (lane == h + FOX_HEADS) | (lane == h + 2 * FOX_HEADS)
        qx = jnp.where(pick, 1.0, 0.0).astype(_BF16)

    def q_rows(qs):
        q = q_ref[qs:qs + t, :]
        return jnp.concatenate([q, qx], axis=1) if one_hot_q else q

    def scores(q, ks, kw):
        k = jnp.concatenate([k_ref[ks:ks + kw, :], kx_ref[ks:ks + kw, :]], axis=1)
        return lax.dot_general(q, k, (((1,), (1,)), ((), ())), preferred_element_type=_F32)

    def update(s, rows, ks, m_scr, acc_scr):
        kw = s.shape[1]
        m = m_scr[rows, :]
        m_new = jnp.maximum(m, jnp.max(s, axis=1, keepdims=True))
        alpha = jnp.exp2(m - m_new)
        m_scr[rows, :] = m_new
        p = jnp.concatenate(
            [jnp.exp2(s[:, c * LANES:(c + 1) * LANES] - m_new) for c in range(kw // LANES)], axis=1)
        v = jnp.concatenate([v_ref[ks:ks + kw, :], ones[0:kw]], axis=1)
        pv = jnp.dot(p.astype(_BF16), v, preferred_element_type=_F32)
        for c in range(2):
            sl = slice(c * LANES, (c + 1) * LANES)
            acc_scr[rows, sl] = alpha * acc_scr[rows, sl] + pv[:, sl]

    tiles = [(i, j) for i in range(nt) for j in range(i + 1)]
    half = t // 2
    top, bottom, full = slice(0, half), slice(half, t), slice(0, t)

    def issue(n):
        i, j = tiles[n]
        buf = s_scr[n % len(s_scr)]
        q = q_rows(i * t)
        if j == i:
            buf[top, 0:half] = scores(q[top], j * t, half)
            buf[bottom, :] = scores(q[bottom], j * t, t)
        else:
            buf[...] = scores(q, j * t, t)

    issue(0)
    for n, (i, j) in enumerate(tiles):
        m_scr, acc_scr = state_scr[2 * (i % 2)], state_scr[2 * (i % 2) + 1]
        if j == 0:
            m_scr[...] = jnp.full(m_scr.shape, MASK_VALUE, _F32)
            acc_scr[...] = jnp.zeros_like(acc_scr)
        if n + 1 < len(tiles):
            issue(n + 1)
        buf = s_scr[n % len(s_scr)]
        if j == i:
            update(jnp.where(allowed[top, 0:half], buf[top, 0:half], MASK_VALUE), top, j * t, m_scr, acc_scr)
            update(jnp.where(allowed[bottom, :], buf[bottom, :], MASK_VALUE), bottom, j * t, m_scr, acc_scr)
        else:
            update(buf[...], full, j * t, m_scr, acc_scr)
        if j == i:
            o_ref[i * t:(i + 1) * t, :] = (
                acc_scr[:, 0:LANES] / acc_scr[:, LANES:2 * LANES]).astype(o_ref.dtype)


def _attention(q, q_spec, k, k_spec, kx, v, v_spec, batch, seq, heads, causal_unit, one_hot_q, name):
    kern = functools.partial(_attn_kernel, causal_unit=causal_unit, one_hot_q=one_hot_q)
    return pl.pallas_call(
        kern,
        grid=(batch, heads),
        in_specs=[
            q_spec,
            k_spec,
            pl.BlockSpec((seq, LANES), lambda b, h: (b, 0)),
            v_spec,
        ],
        out_specs=pl.BlockSpec((seq, LANES), lambda b, h: (b, h)),
        out_shape=jax.ShapeDtypeStruct((batch * seq, heads * LANES), _BF16),
        scratch_shapes=(
            [pltpu.VMEM((ATT_T, ATT_T), _F32)] * ATT_SCORE_BUFS
            + [pltpu.VMEM((ATT_T, LANES), _F32), pltpu.VMEM((ATT_T, 2 * LANES), _F32)] * 2),
        compiler_params=pltpu.CompilerParams(
            dimension_semantics=("arbitrary", "arbitrary"),
            vmem_limit_bytes=40 * 1024 * 1024),
        name=name,
    )(q, k, kx, v)


def _out_proj_kernel(om_ref, of_ref, gm_ref, gf_ref, w_ref, x_ref, o_ref):
    om = om_ref[...].astype(_F32)
    of = of_ref[...].astype(_F32)
    ym = jnp.dot((om * gm_ref[...]).astype(_BF16), w_ref[0:HEAD_WIDTH, :], preferred_element_type=_F32)
    yf = jnp.dot((of * gf_ref[...]).astype(_BF16), w_ref[HEAD_WIDTH:2 * HEAD_WIDTH, :],
                 preferred_element_type=_F32)
    o_ref[...] = x_ref[...] + _scale_rows(ym, _inv_rms(om)) + _scale_rows(yf, _inv_rms(of))


def _out_proj(o_mla, o_fox, gm, gf, w_out, x2d):
    T = x2d.shape[0]
    tm = OUT_TM
    return pl.pallas_call(
        _out_proj_kernel,
        grid=(T // tm,),
        in_specs=[
            pl.BlockSpec((tm, HEAD_WIDTH), lambda i: (i, 0)),
            pl.BlockSpec((tm, HEAD_WIDTH), lambda i: (i, 0)),
            pl.BlockSpec((1, HEAD_WIDTH), lambda i: (0, 0)),
            pl.BlockSpec((1, HEAD_WIDTH), lambda i: (0, 0)),
            pl.BlockSpec((2 * HEAD_WIDTH, D_MODEL), lambda i: (0, 0)),
            pl.BlockSpec((tm, D_MODEL), lambda i: (i, 0)),
        ],
        out_specs=pl.BlockSpec((tm, D_MODEL), lambda i: (i, 0)),
        out_shape=jax.ShapeDtypeStruct((T, D_MODEL), _F32),
        compiler_params=pltpu.CompilerParams(
            dimension_semantics=("arbitrary",),
            vmem_limit_bytes=48 * 1024 * 1024),
        name="out_proj",
    )(o_mla, o_fox, gm, gf, w_out, x2d)


def _ffn_kernel(x_ref, g_ref, wg_ref, wu_ref, wd_ref, gfin_ref, o_ref, h_scr, inv_scr):
    j = pl.program_id(1)

    def hidden_tile(h, inv):
        gate = _scale_rows(jnp.dot(h, wg_ref[...], preferred_element_type=_F32), inv)
        up = _scale_rows(jnp.dot(h, wu_ref[...], preferred_element_type=_F32), inv)
        a = (gate * (1.0 / (1.0 + jnp.exp(-gate))) * up).astype(_BF16)
        return jnp.dot(a, wd_ref[...], preferred_element_type=_F32)

    @pl.when(j == 0)
    def _():
        x = x_ref[...]
        h = (x * g_ref[...]).astype(_BF16)
        h_scr[...] = h
        inv = _inv_rms(x)
        inv_scr[...] = inv
        o_ref[...] = hidden_tile(h, inv)

    last = pl.num_programs(1) - 1

    @pl.when((j > 0) & (j < last))
    def _():
        o_ref[...] += hidden_tile(h_scr[...], inv_scr[...])

    @pl.when(j == last)
    def _():
        y = x_ref[...] + o_ref[...] + hidden_tile(h_scr[...], inv_scr[...])
        o_ref[...] = _rms_scale(y) * gfin_ref[...]


def _ffn(x1, g, w_gate, w_up, w_down, g_final):
    T = x1.shape[0]
    tm, th = FFN_TM, FFN_TH
    nh = FFN_HIDDEN // th

    def hid(i, j):
        return jnp.where(i % 2 == 0, j, nh - 1 - j)

    return pl.pallas_call(
        _ffn_kernel,
        grid=(T // tm, nh),
        in_specs=[
            pl.BlockSpec((tm, D_MODEL), lambda i, j: (i, 0)),
            pl.BlockSpec((1, D_MODEL), lambda i, j: (0, 0)),
            pl.BlockSpec((D_MODEL, th), lambda i, j: (0, hid(i, j))),
            pl.BlockSpec((D_MODEL, th), lambda i, j: (0, hid(i, j))),
            pl.BlockSpec((th, D_MODEL), lambda i, j: (hid(i, j), 0)),
            pl.BlockSpec((1, D_MODEL), lambda i, j: (0, 0)),
        ],
        out_specs=pl.BlockSpec((tm, D_MODEL), lambda i, j: (i, 0)),
        out_shape=jax.ShapeDtypeStruct((T, D_MODEL), _F32),
        scratch_shapes=[pltpu.VMEM((tm, D_MODEL), _BF16), pltpu.VMEM((tm, LANES), _F32)],
        compiler_params=pltpu.CompilerParams(
            dimension_semantics=("arbitrary", "arbitrary"),
            vmem_limit_bytes=63 * 1024 * 1024),
        name="ffn",
    )(x1, g, w_gate, w_up, w_down, g_final)


def _prep_w_in(w):
    w = w.astype(_BF16)
    lat = Q_LORA + KV_LORA
    kr = w[:, lat:lat + QK_ROPE]
    fox0 = lat + QK_ROPE
    f = w[:, fox0 + 3 * HEAD_WIDTH:]
    pad = jnp.zeros((w.shape[0], LANES - 3 * FOX_HEADS), w.dtype)
    misc = jnp.concatenate([f, f, f, pad, w[:, :lat], kr, kr], axis=1)
    return misc, w[:, fox0:fox0 + 3 * HEAD_WIDTH]


def _prep_w_uq(w):
    w3 = w.reshape(Q_LORA, MLA_HEADS, QK_NOPE + QK_ROPE)
    w3 = jnp.concatenate([w3, w3[:, :, QK_NOPE:]], axis=2)
    return w3.reshape(Q_LORA, MLA_HEADS * QK_WIDTH).astype(_BF16)


def kernel(x, positions, g_attn_norm, w_in, b_forget, g_q_lat, w_uq, g_kv_lat, w_ukv, g_out_mla, g_out_fox, w_out, g_ffn_norm, w_gate, w_up, w_down, g_final_norm):
    B, S, D = x.shape
    T = B * S
    assert w_in.shape[0] == 1, "one layer (DEPTH == 1) is supported"
    inv_freq = ROPE_THETA ** (-jnp.arange(0, QK_ROPE, 2, dtype=_F32) / QK_ROPE)
    invf_row = jnp.tile(inv_freq, LANES // (QK_ROPE // 2)).reshape(1, LANES)
    pos_f = positions.astype(_F32).reshape(T, 1)
    x2d = x.reshape(T, D)

    b_row = jnp.concatenate(
        [b_forget[0]] * 3 + [jnp.zeros((LANES - 3 * FOX_HEADS,), _F32)]).reshape(1, LANES)
    w_misc, w_fox = _prep_w_in(w_in[0])
    misc, dec, rot, fox = _in_proj(x2d, g_attn_norm[0].reshape(1, D), w_misc, w_fox, b_row,
                                   pos_f, invf_row, S)
    mq, mkn, mkpe, mv = _mla_up(misc, rot, g_q_lat[0].reshape(1, Q_LORA),
                                g_kv_lat[0].reshape(1, KV_LORA), _prep_w_uq(w_uq[0]), w_ukv[0].astype(_BF16))

    def head_spec(width):
        return pl.BlockSpec((None, S, width), lambda b, h: (h, b, 0))

    def slab_spec(slab):
        return pl.BlockSpec((None, None, S, LANES), lambda b, h: (slab, h, b, 0))

    o_mla = _attention(mq, head_spec(QK_WIDTH), mkn, head_spec(LANES), mkpe, mv, head_spec(LANES),
                       B, S, MLA_HEADS, CHUNK, False, "mla_attn")
    o_fox = _attention(fox, slab_spec(0), fox, slab_spec(1), dec, fox, slab_spec(2),
                       B, S, FOX_HEADS, 1, True, "fox_attn")
    x1 = _out_proj(o_mla, o_fox, g_out_mla[0].reshape(1, HEAD_WIDTH),
                   g_out_fox[0].reshape(1, HEAD_WIDTH), w_out[0].astype(_BF16), x2d)
    out = _ffn(x1, g_ffn_norm[0].reshape(1, D), w_gate[0].astype(_BF16), w_up[0].astype(_BF16),
               w_down[0].astype(_BF16), g_final_norm.reshape(1, D))
    return out.reshape(B, S, D)
```

```python
import functools
import math

import jax
import jax.numpy as jnp
from jax import lax
from jax.experimental import pallas as pl
from jax.experimental.pallas import tpu as pltpu

D_MODEL = 2048
CHUNK = 64
EPS = 1e-6
ROPE_THETA = 10000.0
MLA_HEADS = 8
Q_LORA = 512
KV_LORA = 256
QK_NOPE = 128
QK_ROPE = 64
V_HEAD = 128
FOX_HEADS = 8
FOX_HEAD_DIM = 128
HEAD_WIDTH = 1024
FFN_HIDDEN = 5632

LANES = 128
QK_WIDTH = 256
LOG2E = math.log2(math.e)
MASK_VALUE = -1e30

IN_TM = 1024
IN_TN = 1024
CUM_ROWS = 256
UP_TM = 1024
ATT_T = 512
ATT_SCORE_BUFS = 3
OUT_TM = 512
FFN_TM = 1024
FFN_TH = 512

_MIB = 1024 * 1024
IN_PROJ_VMEM = 60 * _MIB
MLA_UP_VMEM = 56 * _MIB
ATTENTION_VMEM = 40 * _MIB
OUT_PROJ_VMEM = 48 * _MIB
FFN_VMEM = 63 * _MIB

_BF16 = jnp.bfloat16
_F32 = jnp.float32


def _rms_scale(x):
    return x * lax.rsqrt(jnp.mean(x * x, axis=-1, keepdims=True) + EPS)


def _inv_rms(x):
    inv = lax.rsqrt(jnp.mean(x * x, axis=-1, keepdims=True) + EPS)
    return jnp.broadcast_to(inv, (x.shape[0], LANES))


def _scale_rows(y, inv):
    return jnp.concatenate(
        [y[:, c * LANES:(c + 1) * LANES] * inv for c in range(y.shape[1] // LANES)], axis=1)


def _sincos(x):
    k = jnp.floor(x * (2.0 / math.pi) + 0.5)
    r = ((x - k * 1.5703125) - k * 4.837512969970703125e-4) - k * 7.54978995489188216e-8
    r2 = r * r
    sin_r = r + r * r2 * (-1.6666654611e-1 + r2 * (8.3321608736e-3 + r2 * -1.9515295891e-4))
    cos_r = 1.0 - 0.5 * r2 + r2 * r2 * (4.166664568298827e-2 + r2 * (-1.388731625493765e-3
                                                                    + r2 * 2.443315711809948e-5))
    half_k = jnp.floor(k * 0.5)
    odd = k - 2.0 * half_k
    sign = 1.0 - 2.0 * (half_k - 2.0 * jnp.floor(half_k * 0.5))
    return sign * (sin_r + odd * (cos_r - sin_r)), sign * (cos_r - odd * (cos_r + sin_r))


def _split3_bf16(x):
    hi = x.astype(_BF16)
    r = x - hi.astype(_F32)
    mid = r.astype(_BF16)
    lo = (r - mid.astype(_F32)).astype(_BF16)
    return hi, mid, lo


def _in_proj_kernel(x_ref, g_ref, wm_ref, wf_ref, b_ref, pos_ref, invf_ref,
                    misc_ref, dec_ref, rot_ref, qkv_ref,
                    h_scr, inv_scr, f_scr, carry_scr, *, tiles_per_seq, q_scale):
    i = pl.program_id(0)
    j = pl.program_id(1)
    tm = x_ref.shape[0]
    lane = lax.broadcasted_iota(jnp.int32, (tm, LANES), 1)

    @pl.when(j == 0)
    def _():
        x = x_ref[...]
        h = (x * g_ref[...]).astype(_BF16)
        h_scr[...] = h
        inv = _inv_rms(x)
        inv_scr[...] = inv
        acc = _scale_rows(jnp.dot(h, wm_ref[...], preferred_element_type=_F32), inv)
        misc_ref[...] = acc.astype(_BF16)
        f_scr[...] = acc[:, 0:LANES] + b_ref[...]

    def decay_columns():
        f = f_scr[...]
        log_f = jnp.minimum(f, 0.0) - jnp.log1p(jnp.exp(-jnp.abs(f)))
        hi, mid, lo = _split3_bf16(log_f)
        zero = jnp.zeros_like(hi)
        parts = jnp.where(lane < 8, hi, jnp.where(lane < 16, mid, jnp.where(lane < 24, lo, zero)))
        r = lax.broadcasted_iota(jnp.int32, (CUM_ROWS, CUM_ROWS), 0)
        c = lax.broadcasted_iota(jnp.int32, (CUM_ROWS, CUM_ROWS), 1)
        tri = jnp.where(c <= r, 1.0, 0.0).astype(_BF16)

        carry = jnp.where(i % tiles_per_seq == 0, 0.0, carry_scr[...])
        chunks = []
        for ch in range(tm // CUM_ROWS):
            y = jnp.dot(tri, parts[ch * CUM_ROWS:(ch + 1) * CUM_ROWS], preferred_element_type=_F32)
            y = (y + pltpu.roll(y, 8, 1) + pltpu.roll(y, 16, 1)
                 + pltpu.roll(y, LANES - 8, 1) + pltpu.roll(y, LANES - 16, 1))
            chunks.append(y + carry)
            carry = carry + y[CUM_ROWS - 1:CUM_ROWS, :]
        carry_scr[...] = carry
        cum = jnp.concatenate(chunks, axis=0)
        dhi, dmid, dlo = _split3_bf16(cum * (-LOG2E))
        dec_ref[...] = jnp.where(lane < 8, dhi, jnp.where(lane < 16, dmid, jnp.where(lane < 24, dlo, zero)))

    def store_heads(scale):
        acc = jnp.dot(h_scr[...], wf_ref[...], preferred_element_type=_F32)
        for h in range(FOX_HEADS):
            qkv_ref[0, h] = (acc[:, h * LANES:(h + 1) * LANES] * scale).astype(_BF16)

    slab = _fox_slab(i, j)

    @pl.when((j > 0) & (slab == 0))
    def _():
        store_heads(inv_scr[...] * q_scale)
        sin, cos = _sincos(pos_ref[...] * invf_ref[...])
        rot_ref[:, 0:LANES] = cos
        rot_ref[:, LANES:2 * LANES] = jnp.where((lane % QK_ROPE) < QK_ROPE // 2, -sin, sin)

    @pl.when((j > 0) & (slab == 1))
    def _():
        store_heads(inv_scr[...])
        decay_columns()

    @pl.when((j > 0) & (slab == 2))
    def _():
        store_heads(inv_scr[...])


def _fox_slab(i, j):
    return jnp.where(i % 2 == 0, jnp.maximum(j - 1, 0), jnp.minimum(3 - j, 2))


def _in_proj(x2d, g, w_misc, w_fox, b_row, pos_f, invf_row, seq):
    T = x2d.shape[0]
    tm = IN_TM
    grid = (T // tm, 1 + w_fox.shape[1] // IN_TN)
    kern = functools.partial(_in_proj_kernel, tiles_per_seq=seq // tm,
                             q_scale=LOG2E / math.sqrt(FOX_HEAD_DIM))
    return pl.pallas_call(
        kern,
        grid=grid,
        in_specs=[
            pl.BlockSpec((tm, D_MODEL), lambda i, j: (i, 0)),
            pl.BlockSpec((1, D_MODEL), lambda i, j: (0, 0)),
            pl.BlockSpec((D_MODEL, IN_TN), lambda i, j: (0, 0)),
            pl.BlockSpec((D_MODEL, IN_TN), lambda i, j: (0, _fox_slab(i, j))),
            pl.BlockSpec((1, LANES), lambda i, j: (0, 0)),
            pl.BlockSpec((tm, 1), lambda i, j: (i, 0)),
            pl.BlockSpec((1, LANES), lambda i, j: (0, 0)),
        ],
        out_specs=[
            pl.BlockSpec((tm, IN_TN), lambda i, j: (i, 0)),
            pl.BlockSpec((tm, LANES), lambda i, j: (i, 0)),
            pl.BlockSpec((tm, 2 * LANES), lambda i, j: (i, 0)),
            pl.BlockSpec((1, FOX_HEADS, tm, LANES), lambda i, j: (_fox_slab(i, j), 0, i, 0)),
        ],
        out_shape=[
            jax.ShapeDtypeStruct((T, IN_TN), _BF16),
            jax.ShapeDtypeStruct((T, LANES), _BF16),
            jax.ShapeDtypeStruct((T, 2 * LANES), _F32),
            jax.ShapeDtypeStruct((3, FOX_HEADS, T, LANES), _BF16),
        ],
        scratch_shapes=[
            pltpu.VMEM((tm, D_MODEL), _BF16),
            pltpu.VMEM((tm, LANES), _F32),
            pltpu.VMEM((tm, LANES), _F32),
            pltpu.VMEM((1, LANES), _F32),
        ],
        compiler_params=pltpu.CompilerParams(
            dimension_semantics=("arbitrary", "arbitrary"),
            vmem_limit_bytes=IN_PROJ_VMEM),
        name="in_proj",
    )(x2d, g, w_misc, w_fox, b_row, pos_f, invf_row)


def _mla_up_kernel(misc_ref, rot_ref, gq_ref, gkv_ref, wuq_ref, wukv_ref,
                   q_ref, kn_ref, kpe_ref, v_ref, *, q_scale):
    tm = misc_ref.shape[0]
    q_lat = misc_ref[:, LANES:LANES + Q_LORA].astype(_F32)
    kv_lat = misc_ref[:, LANES + Q_LORA:LANES + Q_LORA + KV_LORA].astype(_F32)
    k_rope = misc_ref[:, LANES + Q_LORA + KV_LORA:IN_TN].astype(_F32)

    qn = (q_lat * gq_ref[...]).astype(_BF16)
    kvn = (kv_lat * gkv_ref[...]).astype(_BF16)
    q = jnp.dot(qn, wuq_ref[...], preferred_element_type=_F32)
    kv = jnp.dot(kvn, wukv_ref[...], preferred_element_type=_F32)
    q_factor = _inv_rms(q_lat) * q_scale
    kv_factor = _inv_rms(kv_lat)

    cos = rot_ref[:, 0:LANES]
    sin_signed = rot_ref[:, LANES:2 * LANES]
    lane = lax.broadcasted_iota(jnp.int32, (tm, LANES), 1)

    def rope(p):
        return p * cos + pltpu.roll(p, QK_ROPE // 2, 1) * sin_signed

    kpe_ref[...] = jnp.where(lane < QK_ROPE, rope(k_rope), 0.0).astype(_BF16)
    for h in range(MLA_HEADS):
        base = h * QK_WIDTH
        q_ref[h, :, 0:LANES] = (q[:, base:base + LANES] * q_factor).astype(_BF16)
        q_ref[h, :, LANES:QK_WIDTH] = (rope(q[:, base + LANES:base + QK_WIDTH]) * q_factor).astype(_BF16)
        kn_ref[h] = (kv[:, base:base + LANES] * kv_factor).astype(_BF16)
        v_ref[h] = (kv[:, base + LANES:base + QK_WIDTH] * kv_factor).astype(_BF16)


def _mla_up(misc, rot, gq, gkv, w_uq_r, w_ukv):
    T = misc.shape[0]
    tm = UP_TM
    kern = functools.partial(_mla_up_kernel, q_scale=LOG2E / math.sqrt(QK_NOPE + QK_ROPE))
    return pl.pallas_call(
        kern,
        grid=(T // tm,),
        in_specs=[
            pl.BlockSpec((tm, IN_TN), lambda i: (i, 0)),
            pl.BlockSpec((tm, 2 * LANES), lambda i: (i, 0)),
            pl.BlockSpec((1, Q_LORA), lambda i: (0, 0)),
            pl.BlockSpec((1, KV_LORA), lambda i: (0, 0)),
            pl.BlockSpec((Q_LORA, MLA_HEADS * QK_WIDTH), lambda i: (0, 0)),
            pl.BlockSpec((KV_LORA, MLA_HEADS * QK_WIDTH), lambda i: (0, 0)),
        ],
        out_specs=[
            pl.BlockSpec((MLA_HEADS, tm, QK_WIDTH), lambda i: (0, i, 0)),
            pl.BlockSpec((MLA_HEADS, tm, LANES), lambda i: (0, i, 0)),
            pl.BlockSpec((tm, LANES), lambda i: (i, 0)),
            pl.BlockSpec((MLA_HEADS, tm, LANES), lambda i: (0, i, 0)),
        ],
        out_shape=[
            jax.ShapeDtypeStruct((MLA_HEADS, T, QK_WIDTH), _BF16),
            jax.ShapeDtypeStruct((MLA_HEADS, T, LANES), _BF16),
            jax.ShapeDtypeStruct((T, LANES), _BF16),
            jax.ShapeDtypeStruct((MLA_HEADS, T, LANES), _BF16),
        ],
        compiler_params=pltpu.CompilerParams(
            dimension_semantics=("arbitrary",),
            vmem_limit_bytes=MLA_UP_VMEM),
        name="mla_up",
    )(misc, rot, gq, gkv, w_uq_r, w_ukv)


def _attn_kernel(q_ref, k_ref, kx_ref, v_ref, o_ref, *scratch, causal_unit, one_hot_q):
    s_scr, state_scr = scratch[:ATT_SCORE_BUFS], scratch[ATT_SCORE_BUFS:]
    seq = k_ref.shape[0]
    t = ATT_T
    nt = seq // t
    row = lax.broadcasted_iota(jnp.int32, (t, t), 0)
    col = lax.broadcasted_iota(jnp.int32, (t, t), 1)
    if causal_unit == 1:
        allowed = col <= row
    else:
        allowed = (col // causal_unit) <= (row // causal_unit)
    ones = jnp.ones((t, LANES), _BF16)
    if one_hot_q:
        h = pl.program_id(1)
        lane = lax.broadcasted_iota(jnp.int32, (t, LANES), 1)
        pick = (lane == h) | (lane == h + FOX_HEADS) | (lane == h + 2 * FOX_HEADS)
        qx = jnp.where(pick, 1.0, 0.0).astype(_BF16)

    def q_rows(qs):
        q = q_ref[qs:qs + t, :]
        return jnp.concatenate([q, qx], axis=1) if one_hot_q else q

    def scores(q, ks, kw):
        k = jnp.concatenate([k_ref[ks:ks + kw, :], kx_ref[ks:ks + kw, :]], axis=1)
        return lax.dot_general(q, k, (((1,), (1,)), ((), ())), preferred_element_type=_F32)

    def update(s, rows, ks, m_scr, acc_scr):
        kw = s.shape[1]
        m = m_scr[rows, :]
        m_new = jnp.maximum(m, jnp.max(s, axis=1, keepdims=True))
        alpha = jnp.exp2(m - m_new)
        m_scr[rows, :] = m_new
        p = jnp.concatenate(
            [jnp.exp2(s[:, c * LANES:(c + 1) * LANES] - m_new) for c in range(kw // LANES)], axis=1)
        v = jnp.concatenate([v_ref[ks:ks + kw, :], ones[0:kw]], axis=1)
        pv = jnp.dot(p.astype(_BF16), v, preferred_element_type=_F32)
        for c in range(2):
            sl = slice(c * LANES, (c + 1) * LANES)
            acc_scr[rows, sl] = alpha * acc_scr[rows, sl] + pv[:, sl]

    tiles = [(i, j) for i in range(nt) for j in range(i + 1)]
    half = t // 2
    top, bottom, full = slice(0, half), slice(half, t), slice(0, t)

    def issue(n):
        i, j = tiles[n]
        buf = s_scr[n % len(s_scr)]
        q = q_rows(i * t)
        if j == i:
            buf[top, 0:half] = scores(q[top], j * t, half)
            buf[bottom, :] = scores(q[bottom], j * t, t)
        else:
            buf[...] = scores(q, j * t, t)

    issue(0)
    for n, (i, j) in enumerate(tiles):
        m_scr, acc_scr = state_scr[2 * (i % 2)], state_scr[2 * (i % 2) + 1]
        if j == 0:
            m_scr[...] = jnp.full(m_scr.shape, MASK_VALUE, _F32)
            acc_scr[...] = jnp.zeros_like(acc_scr)
        if n + 1 < len(tiles):
            issue(n + 1)
        buf = s_scr[n % len(s_scr)]
        if j == i:
            update(jnp.where(allowed[top, 0:half], buf[top, 0:half], MASK_VALUE), top, j * t, m_scr, acc_scr)
            update(jnp.where(allowed[bottom, :], buf[bottom, :], MASK_VALUE), bottom, j * t, m_scr, acc_scr)
        else:
            update(buf[...], full, j * t, m_scr, acc_scr)
        if j == i:
            o_ref[i * t:(i + 1) * t, :] = (
                acc_scr[:, 0:LANES] / acc_scr[:, LANES:2 * LANES]).astype(o_ref.dtype)


def _attention(q, q_spec, k, k_spec, kx, v, v_spec, batch, seq, heads, causal_unit, one_hot_q, name):
    kern = functools.partial(_attn_kernel, causal_unit=causal_unit, one_hot_q=one_hot_q)
    return pl.pallas_call(
        kern,
        grid=(batch, heads),
        in_specs=[
            q_spec,
            k_spec,
            pl.BlockSpec((seq, LANES), lambda b, h: (b, 0)),
            v_spec,
        ],
        out_specs=pl.BlockSpec((seq, LANES), lambda b, h: (b, h)),
        out_shape=jax.ShapeDtypeStruct((batch * seq, heads * LANES), _BF16),
        scratch_shapes=(
            [pltpu.VMEM((ATT_T, ATT_T), _F32)] * ATT_SCORE_BUFS
            + [pltpu.VMEM((ATT_T, LANES), _F32), pltpu.VMEM((ATT_T, 2 * LANES), _F32)] * 2),
        compiler_params=pltpu.CompilerParams(
            dimension_semantics=("arbitrary", "arbitrary"),
            vmem_limit_bytes=ATTENTION_VMEM),
        name=name,
    )(q, k, kx, v)


def _out_proj_kernel(om_ref, of_ref, gm_ref, gf_ref, w_ref, x_ref, o_ref):
    om = om_ref[...].astype(_F32)
    of = of_ref[...].astype(_F32)
    ym = jnp.dot((om * gm_ref[...]).astype(_BF16), w_ref[0:HEAD_WIDTH, :], preferred_element_type=_F32)
    yf = jnp.dot((of * gf_ref[...]).astype(_BF16), w_ref[HEAD_WIDTH:2 * HEAD_WIDTH, :],
                 preferred_element_type=_F32)
    o_ref[...] = x_ref[...] + _scale_rows(ym, _inv_rms(om)) + _scale_rows(yf, _inv_rms(of))


def _out_proj(o_mla, o_fox, gm, gf, w_out, x2d):
    T = x2d.shape[0]
    tm = OUT_TM
    return pl.pallas_call(
        _out_proj_kernel,
        grid=(T // tm,),
        in_specs=[
            pl.BlockSpec((tm, HEAD_WIDTH), lambda i: (i, 0)),
            pl.BlockSpec((tm, HEAD_WIDTH), lambda i: (i, 0)),
            pl.BlockSpec((1, HEAD_WIDTH), lambda i: (0, 0)),
            pl.BlockSpec((1, HEAD_WIDTH), lambda i: (0, 0)),
            pl.BlockSpec((2 * HEAD_WIDTH, D_MODEL), lambda i: (0, 0)),
            pl.BlockSpec((tm, D_MODEL), lambda i: (i, 0)),
        ],
        out_specs=pl.BlockSpec((tm, D_MODEL), lambda i: (i, 0)),
        out_shape=jax.ShapeDtypeStruct((T, D_MODEL), _F32),
        compiler_params=pltpu.CompilerParams(
            dimension_semantics=("arbitrary",),
            vmem_limit_bytes=OUT_PROJ_VMEM),
        name="out_proj",
    )(o_mla, o_fox, gm, gf, w_out, x2d)


def _ffn_kernel(x_ref, g_ref, wg_ref, wu_ref, wd_ref, gfin_ref, o_ref, h_scr, inv_scr):
    j = pl.program_id(1)

    def hidden_tile(h, inv):
        gate = _scale_rows(jnp.dot(h, wg_ref[...], preferred_element_type=_F32), inv)
        up = _scale_rows(jnp.dot(h, wu_ref[...], preferred_element_type=_F32), inv)
        a = (gate * (1.0 / (1.0 + jnp.exp(-gate))) * up).astype(_BF16)
        return jnp.dot(a, wd_ref[...], preferred_element_type=_F32)

    @pl.when(j == 0)
    def _():
        x = x_ref[...]
        h = (x * g_ref[...]).astype(_BF16)
        h_scr[...] = h
        inv = _inv_rms(x)
        inv_scr[...] = inv
        o_ref[...] = hidden_tile(h, inv)

    last = pl.num_programs(1) - 1

    @pl.when((j > 0) & (j < last))
    def _():
        o_ref[...] += hidden_tile(h_scr[...], inv_scr[...])

    @pl.when(j == last)
    def _():
        y = x_ref[...] + o_ref[...] + hidden_tile(h_scr[...], inv_scr[...])
        o_ref[...] = _rms_scale(y) * gfin_ref[...]


def _ffn(x1, g, w_gate, w_up, w_down, g_final):
    T = x1.shape[0]
    tm, th = FFN_TM, FFN_TH
    nh = FFN_HIDDEN // th

    def hid(i, j):
        return jnp.where(i % 2 == 0, j, nh - 1 - j)

    return pl.pallas_call(
        _ffn_kernel,
        grid=(T // tm, nh),
        in_specs=[
            pl.BlockSpec((tm, D_MODEL), lambda i, j: (i, 0)),
            pl.BlockSpec((1, D_MODEL), lambda i, j: (0, 0)),
            pl.BlockSpec((D_MODEL, th), lambda i, j: (0, hid(i, j))),
            pl.BlockSpec((D_MODEL, th), lambda i, j: (0, hid(i, j))),
            pl.BlockSpec((th, D_MODEL), lambda i, j: (hid(i, j), 0)),
            pl.BlockSpec((1, D_MODEL), lambda i, j: (0, 0)),
        ],
        out_specs=pl.BlockSpec((tm, D_MODEL), lambda i, j: (i, 0)),
        out_shape=jax.ShapeDtypeStruct((T, D_MODEL), _F32),
        scratch_shapes=[pltpu.VMEM((tm, D_MODEL), _BF16), pltpu.VMEM((tm, LANES), _F32)],
        compiler_params=pltpu.CompilerParams(
            dimension_semantics=("arbitrary", "arbitrary"),
            vmem_limit_bytes=FFN_VMEM),
        name="ffn",
    )(x1, g, w_gate, w_up, w_down, g_final)


def _prep_w_in(w):
    w = w.astype(_BF16)
    lat = Q_LORA + KV_LORA
    kr = w[:, lat:lat + QK_ROPE]
    fox0 = lat + QK_ROPE
    f = w[:, fox0 + 3 * HEAD_WIDTH:]
    pad = jnp.zeros((w.shape[0], LANES - 3 * FOX_HEADS), w.dtype)
    misc = jnp.concatenate([f, f, f, pad, w[:, :lat], kr, kr], axis=1)
    return misc, w[:, fox0:fox0 + 3 * HEAD_WIDTH]


def _prep_w_uq(w):
    w3 = w.reshape(Q_LORA, MLA_HEADS, QK_NOPE + QK_ROPE)
    w3 = jnp.concatenate([w3, w3[:, :, QK_NOPE:]], axis=2)
    return w3.reshape(Q_LORA, MLA_HEADS * QK_WIDTH).astype(_BF16)


def kernel(x, positions, g_attn_norm, w_in, b_forget, g_q_lat, w_uq, g_kv_lat, w_ukv, g_out_mla, g_out_fox, w_out, g_ffn_norm, w_gate, w_up, w_down, g_final_norm):
    B, S, D = x.shape
    T = B * S
    assert w_in.shape[0] == 1, "one layer (DEPTH == 1) is supported"
    inv_freq = ROPE_THETA ** (-jnp.arange(0, QK_ROPE, 2, dtype=_F32) / QK_ROPE)
    invf_row = jnp.tile(inv_freq, LANES // (QK_ROPE // 2)).reshape(1, LANES)
    pos_f = positions.astype(_F32).reshape(T, 1)
    x2d = x.reshape(T, D)

    b_row = jnp.concatenate(
        [b_forget[0]] * 3 + [jnp.zeros((LANES - 3 * FOX_HEADS,), _F32)]).reshape(1, LANES)
    w_misc, w_fox = _prep_w_in(w_in[0])
    misc, dec, rot, fox = _in_proj(x2d, g_attn_norm[0].reshape(1, D), w_misc, w_fox, b_row,
                                   pos_f, invf_row, S)
    mq, mkn, mkpe, mv = _mla_up(misc, rot, g_q_lat[0].reshape(1, Q_LORA),
                                g_kv_lat[0].reshape(1, KV_LORA), _prep_w_uq(w_uq[0]), w_ukv[0].astype(_BF16))

    def head_spec(width):
        return pl.BlockSpec((None, S, width), lambda b, h: (h, b, 0))

    def slab_spec(slab):
        return pl.BlockSpec((None, None, S, LANES), lambda b, h: (slab, h, b, 0))

    o_mla = _attention(mq, head_spec(QK_WIDTH), mkn, head_spec(LANES), mkpe, mv, head_spec(LANES),
                       B, S, MLA_HEADS, CHUNK, False, "mla_attn")
    o_fox = _attention(fox, slab_spec(0), fox, slab_spec(1), dec, fox, slab_spec(2),
                       B, S, FOX_HEADS, 1, True, "fox_attn")
    x1 = _out_proj(o_mla, o_fox, g_out_mla[0].reshape(1, HEAD_WIDTH),
                   g_out_fox[0].reshape(1, HEAD_WIDTH), w_out[0].astype(_BF16), x2d)
    out = _ffn(x1, g_ffn_norm[0].reshape(1, D), w_gate[0].astype(_BF16), w_up[0].astype(_BF16),
               w_down[0].astype(_BF16), g_final_norm.reshape(1, D))
    return out.reshape(B, S, D)
```

```python
import functools
import math

import jax
import jax.numpy as jnp
from jax import lax
from jax.experimental import pallas as pl
from jax.experimental.pallas import tpu as pltpu

D_MODEL = 2048
CHUNK = 64
EPS = 1e-6
ROPE_THETA = 10000.0
MLA_HEADS = 8
Q_LORA = 512
KV_LORA = 256
QK_NOPE = 128
QK_ROPE = 64
V_HEAD = 128
FOX_HEADS = 8
FOX_HEAD_DIM = 128
HEAD_WIDTH = 1024
FFN_HIDDEN = 5632

LANES = 128
QK_WIDTH = 256
LOG2E = math.log2(math.e)
MASK_VALUE = -1e30

IN_TM = 1024
IN_TN = 1024
CUM_ROWS = 256
UP_TM = 1024
ATT_T = 512
ATT_SCORE_BUFS = 3
ATT_HEADS = 2
OUT_TM = 512
FFN_TM = 1024
FFN_TH = 512

_MIB = 1024 * 1024
IN_PROJ_VMEM = 60 * _MIB
MLA_UP_VMEM = 56 * _MIB
ATTENTION_VMEM = 40 * _MIB
OUT_PROJ_VMEM = 48 * _MIB
FFN_VMEM = 63 * _MIB

_BF16 = jnp.bfloat16
_F32 = jnp.float32


def _rms_scale(x):
    return x * lax.rsqrt(jnp.mean(x * x, axis=-1, keepdims=True) + EPS)


def _inv_rms(x):
    inv = lax.rsqrt(jnp.mean(x * x, axis=-1, keepdims=True) + EPS)
    return jnp.broadcast_to(inv, (x.shape[0], LANES))


def _scale_rows(y, inv):
    return jnp.concatenate(
        [y[:, c * LANES:(c + 1) * LANES] * inv for c in range(y.shape[1] // LANES)], axis=1)


def _sincos(x):
    k = jnp.floor(x * (2.0 / math.pi) + 0.5)
    r = ((x - k * 1.5703125) - k * 4.837512969970703125e-4) - k * 7.54978995489188216e-8
    r2 = r * r
    sin_r = r + r * r2 * (-1.6666654611e-1 + r2 * (8.3321608736e-3 + r2 * -1.9515295891e-4))
    cos_r = 1.0 - 0.5 * r2 + r2 * r2 * (4.166664568298827e-2 + r2 * (-1.388731625493765e-3
                                                                    + r2 * 2.443315711809948e-5))
    half_k = jnp.floor(k * 0.5)
    odd = k - 2.0 * half_k
    sign = 1.0 - 2.0 * (half_k - 2.0 * jnp.floor(half_k * 0.5))
    return sign * (sin_r + odd * (cos_r - sin_r)), sign * (cos_r - odd * (cos_r + sin_r))


def _split3_bf16(x):
    hi = x.astype(_BF16)
    r = x - hi.astype(_F32)
    mid = r.astype(_BF16)
    lo = (r - mid.astype(_F32)).astype(_BF16)
    return hi, mid, lo


def _in_proj_kernel(x_ref, g_ref, wm_ref, wf_ref, b_ref, pos_ref, invf_ref,
                    misc_ref, dec_ref, rot_ref, qkv_ref,
                    h_scr, inv_scr, f_scr, carry_scr, *, tiles_per_seq, q_scale):
    i = pl.program_id(0)
    j = pl.program_id(1)
    tm = x_ref.shape[0]
    lane = lax.broadcasted_iota(jnp.int32, (tm, LANES), 1)

    @pl.when(j == 0)
    def _():
        x = x_ref[...]
        h = (x * g_ref[...]).astype(_BF16)
        h_scr[...] = h
        inv = _inv_rms(x)
        inv_scr[...] = inv
        acc = _scale_rows(jnp.dot(h, wm_ref[...], preferred_element_type=_F32), inv)
        misc_ref[...] = acc.astype(_BF16)
        f_scr[...] = acc[:, 0:LANES] + b_ref[...]

    def decay_columns():
        f = f_scr[...]
        log_f = jnp.minimum(f, 0.0) - jnp.log1p(jnp.exp(-jnp.abs(f)))
        hi, mid, lo = _split3_bf16(log_f)
        zero = jnp.zeros_like(hi)
        parts = jnp.where(lane < 8, hi, jnp.where(lane < 16, mid, jnp.where(lane < 24, lo, zero)))
        r = lax.broadcasted_iota(jnp.int32, (CUM_ROWS, CUM_ROWS), 0)
        c = lax.broadcasted_iota(jnp.int32, (CUM_ROWS, CUM_ROWS), 1)
        tri = jnp.where(c <= r, 1.0, 0.0).astype(_BF16)

        carry = jnp.where(i % tiles_per_seq == 0, 0.0, carry_scr[...])
        chunks = []
        for ch in range(tm // CUM_ROWS):
            y = jnp.dot(tri, parts[ch * CUM_ROWS:(ch + 1) * CUM_ROWS], preferred_element_type=_F32)
            y = (y + pltpu.roll(y, 8, 1) + pltpu.roll(y, 16, 1)
                 + pltpu.roll(y, LANES - 8, 1) + pltpu.roll(y, LANES - 16, 1))
            chunks.append(y + carry)
            carry = carry + y[CUM_ROWS - 1:CUM_ROWS, :]
        carry_scr[...] = carry
        cum = jnp.concatenate(chunks, axis=0)
        dhi, dmid, dlo = _split3_bf16(cum * (-LOG2E))
        dec_ref[...] = jnp.where(lane < 8, dhi, jnp.where(lane < 16, dmid, jnp.where(lane < 24, dlo, zero)))

    def store_heads(scale):
        acc = jnp.dot(h_scr[...], wf_ref[...], preferred_element_type=_F32)
        for h in range(FOX_HEADS):
            qkv_ref[0, h] = (acc[:, h * LANES:(h + 1) * LANES] * scale).astype(_BF16)

    slab = _fox_slab(i, j)

    @pl.when((j > 0) & (slab == 0))
    def _():
        store_heads(inv_scr[...] * q_scale)
        sin, cos = _sincos(pos_ref[...] * invf_ref[...])
        rot_ref[:, 0:LANES] = cos
        rot_ref[:, LANES:2 * LANES] = jnp.where((lane % QK_ROPE) < QK_ROPE // 2, -sin, sin)

    @pl.when((j > 0) & (slab == 1))
    def _():
        store_heads(inv_scr[...])
        decay_columns()

    @pl.when((j > 0) & (slab == 2))
    def _():
        store_heads(inv_scr[...])


def _fox_slab(i, j):
    return jnp.where(i % 2 == 0, jnp.maximum(j - 1, 0), jnp.minimum(3 - j, 2))


def _in_proj(x2d, g, w_misc, w_fox, b_row, pos_f, invf_row, seq):
    T = x2d.shape[0]
    tm = IN_TM
    grid = (T // tm, 1 + w_fox.shape[1] // IN_TN)
    kern = functools.partial(_in_proj_kernel, tiles_per_seq=seq // tm,
                             q_scale=LOG2E / math.sqrt(FOX_HEAD_DIM))
    return pl.pallas_call(
        kern,
        grid=grid,
        in_specs=[
            pl.BlockSpec((tm, D_MODEL), lambda i, j: (i, 0)),
            pl.BlockSpec((1, D_MODEL), lambda i, j: (0, 0)),
            pl.BlockSpec((D_MODEL, IN_TN), lambda i, j: (0, 0)),
            pl.BlockSpec((D_MODEL, IN_TN), lambda i, j: (0, _fox_slab(i, j))),
            pl.BlockSpec((1, LANES), lambda i, j: (0, 0)),
            pl.BlockSpec((tm, 1), lambda i, j: (i, 0)),
            pl.BlockSpec((1, LANES), lambda i, j: (0, 0)),
        ],
        out_specs=[
            pl.BlockSpec((tm, IN_TN), lambda i, j: (i, 0)),
            pl.BlockSpec((tm, LANES), lambda i, j: (i, 0)),
            pl.BlockSpec((tm, 2 * LANES), lambda i, j: (i, 0)),
            pl.BlockSpec((1, FOX_HEADS, tm, LANES), lambda i, j: (_fox_slab(i, j), 0, i, 0)),
        ],
        out_shape=[
            jax.ShapeDtypeStruct((T, IN_TN), _BF16),
            jax.ShapeDtypeStruct((T, LANES), _BF16),
            jax.ShapeDtypeStruct((T, 2 * LANES), _F32),
            jax.ShapeDtypeStruct((3, FOX_HEADS, T, LANES), _BF16),
        ],
        scratch_shapes=[
            pltpu.VMEM((tm, D_MODEL), _BF16),
            pltpu.VMEM((tm, LANES), _F32),
            pltpu.VMEM((tm, LANES), _F32),
            pltpu.VMEM((1, LANES), _F32),
        ],
        compiler_params=pltpu.CompilerParams(
            dimension_semantics=("arbitrary", "arbitrary"),
            vmem_limit_bytes=IN_PROJ_VMEM),
        name="in_proj",
    )(x2d, g, w_misc, w_fox, b_row, pos_f, invf_row)


def _mla_up_kernel(misc_ref, rot_ref, gq_ref, gkv_ref, wuq_ref, wukv_ref,
                   q_ref, kn_ref, kpe_ref, v_ref, *, q_scale):
    tm = misc_ref.shape[0]
    q_lat = misc_ref[:, LANES:LANES + Q_LORA].astype(_F32)
    kv_lat = misc_ref[:, LANES + Q_LORA:LANES + Q_LORA + KV_LORA].astype(_F32)
    k_rope = misc_ref[:, LANES + Q_LORA + KV_LORA:IN_TN].astype(_F32)

    qn = (q_lat * gq_ref[...]).astype(_BF16)
    kvn = (kv_lat * gkv_ref[...]).astype(_BF16)
    q = jnp.dot(qn, wuq_ref[...], preferred_element_type=_F32)
    kv = jnp.dot(kvn, wukv_ref[...], preferred_element_type=_F32)
    q_factor = _inv_rms(q_lat) * q_scale
    kv_factor = _inv_rms(kv_lat)

    cos = rot_ref[:, 0:LANES]
    sin_signed = rot_ref[:, LANES:2 * LANES]
    lane = lax.broadcasted_iota(jnp.int32, (tm, LANES), 1)

    def rope(p):
        return p * cos + pltpu.roll(p, QK_ROPE // 2, 1) * sin_signed

    kpe_ref[...] = jnp.where(lane < QK_ROPE, rope(k_rope), 0.0).astype(_BF16)
    for h in range(MLA_HEADS):
        base = h * QK_WIDTH
        q_ref[h, :, 0:LANES] = (q[:, base:base + LANES] * q_factor).astype(_BF16)
        q_ref[h, :, LANES:QK_WIDTH] = (rope(q[:, base + LANES:base + QK_WIDTH]) * q_factor).astype(_BF16)
        kn_ref[h] = (kv[:, base:base + LANES] * kv_factor).astype(_BF16)
        v_ref[h] = (kv[:, base + LANES:base + QK_WIDTH] * kv_factor).astype(_BF16)


def _mla_up(misc, rot, gq, gkv, w_uq_r, w_ukv):
    T = misc.shape[0]
    tm = UP_TM
    kern = functools.partial(_mla_up_kernel, q_scale=LOG2E / math.sqrt(QK_NOPE + QK_ROPE))
    return pl.pallas_call(
        kern,
        grid=(T // tm,),
        in_specs=[
            pl.BlockSpec((tm, IN_TN), lambda i: (i, 0)),
            pl.BlockSpec((tm, 2 * LANES), lambda i: (i, 0)),
            pl.BlockSpec((1, Q_LORA), lambda i: (0, 0)),
            pl.BlockSpec((1, KV_LORA), lambda i: (0, 0)),
            pl.BlockSpec((Q_LORA, MLA_HEADS * QK_WIDTH), lambda i: (0, 0)),
            pl.BlockSpec((KV_LORA, MLA_HEADS * QK_WIDTH), lambda i: (0, 0)),
        ],
        out_specs=[
            pl.BlockSpec((MLA_HEADS, tm, QK_WIDTH), lambda i: (0, i, 0)),
            pl.BlockSpec((MLA_HEADS, tm, LANES), lambda i: (0, i, 0)),
            pl.BlockSpec((tm, LANES), lambda i: (i, 0)),
            pl.BlockSpec((MLA_HEADS, tm, LANES), lambda i: (0, i, 0)),
        ],
        out_shape=[
            jax.ShapeDtypeStruct((MLA_HEADS, T, QK_WIDTH), _BF16),
            jax.ShapeDtypeStruct((MLA_HEADS, T, LANES), _BF16),
            jax.ShapeDtypeStruct((T, LANES), _BF16),
            jax.ShapeDtypeStruct((MLA_HEADS, T, LANES), _BF16),
        ],
        compiler_params=pltpu.CompilerParams(
            dimension_semantics=("arbitrary",),
            vmem_limit_bytes=MLA_UP_VMEM),
        name="mla_up",
    )(misc, rot, gq, gkv, w_uq_r, w_ukv)


def _attn_kernel(q_ref, k_ref, kx_ref, v_ref, o_ref, *scratch, causal_unit, one_hot_q):
    s_scr, state_scr = scratch[:ATT_SCORE_BUFS], scratch[ATT_SCORE_BUFS:]
    seq = k_ref.shape[1]
    t = ATT_T
    nt = seq // t
    row = lax.broadcasted_iota(jnp.int32, (t, t), 0)
    col = lax.broadcasted_iota(jnp.int32, (t, t), 1)
    if causal_unit == 1:
        allowed = col <= row
    else:
        allowed = (col // causal_unit) <= (row // causal_unit)
    ones = jnp.ones((t, LANES), _BF16)
    lane = lax.broadcasted_iota(jnp.int32, (t, LANES), 1)

    def q_rows(hh, qs):
        q = q_ref[hh, qs:qs + t, :]
        if not one_hot_q:
            return q
        h = ATT_HEADS * pl.program_id(1) + hh
        pick = (lane == h) | (lane == h + FOX_HEADS) | (lane == h + 2 * FOX_HEADS)
        return jnp.concatenate([q, jnp.where(pick, 1.0, 0.0).astype(_BF16)], axis=1)

    def scores(q, hh, ks, kw):
        k = jnp.concatenate([k_ref[hh, ks:ks + kw, :], kx_ref[ks:ks + kw, :]], axis=1)
        return lax.dot_general(q, k, (((1,), (1,)), ((), ())), preferred_element_type=_F32)

    def update(s, hh, rows, ks, m_scr, acc_scr):
        kw = s.shape[1]
        m = m_scr[rows, :]
        m_new = jnp.maximum(m, jnp.max(s, axis=1, keepdims=True))
        alpha = jnp.exp2(m - m_new)
        m_scr[rows, :] = m_new
        p = jnp.concatenate(
            [jnp.exp2(s[:, c * LANES:(c + 1) * LANES] - m_new) for c in range(kw // LANES)], axis=1)
        v = jnp.concatenate([v_ref[hh, ks:ks + kw, :], ones[0:kw]], axis=1)
        pv = jnp.dot(p.astype(_BF16), v, preferred_element_type=_F32)
        for c in range(2):
            sl = slice(c * LANES, (c + 1) * LANES)
            acc_scr[rows, sl] = alpha * acc_scr[rows, sl] + pv[:, sl]

    tiles = [(hh, i, j) for hh in range(ATT_HEADS) for i in range(nt) for j in range(i + 1)]
    half = t // 2
    top, bottom, full = slice(0, half), slice(half, t), slice(0, t)

    def issue(n):
        hh, i, j = tiles[n]
        buf = s_scr[n % len(s_scr)]
        q = q_rows(hh, i * t)
        if j == i:
            buf[top, 0:half] = scores(q[top], hh, j * t, half)
            buf[bottom, :] = scores(q[bottom], hh, j * t, t)
        else:
            buf[...] = scores(q, hh, j * t, t)

    issue(0)
    for n, (hh, i, j) in enumerate(tiles):
        parity = (hh * nt + i) % 2
        m_scr, acc_scr = state_scr[2 * parity], state_scr[2 * parity + 1]
        if j == 0:
            m_scr[...] = jnp.full(m_scr.shape, MASK_VALUE, _F32)
            acc_scr[...] = jnp.zeros_like(acc_scr)
        if n + 1 < len(tiles):
            issue(n + 1)
        buf = s_scr[n % len(s_scr)]
        if j == i:
            update(jnp.where(allowed[top, 0:half], buf[top, 0:half], MASK_VALUE), hh, top, j * t, m_scr, acc_scr)
            update(jnp.where(allowed[bottom, :], buf[bottom, :], MASK_VALUE), hh, bottom, j * t, m_scr, acc_scr)
        else:
            update(buf[...], hh, full, j * t, m_scr, acc_scr)
        if j == i:
            o_ref[i * t:(i + 1) * t, hh * LANES:(hh + 1) * LANES] = (
                acc_scr[:, 0:LANES] / acc_scr[:, LANES:2 * LANES]).astype(o_ref.dtype)


def _attention(q, q_spec, k, k_spec, kx, v, v_spec, batch, seq, heads, causal_unit, one_hot_q, name):
    kern = functools.partial(_attn_kernel, causal_unit=causal_unit, one_hot_q=one_hot_q)
    return pl.pallas_call(
        kern,
        grid=(batch, heads // ATT_HEADS),
        in_specs=[
            q_spec,
            k_spec,
            pl.BlockSpec((seq, LANES), lambda b, h: (b, 0)),
            v_spec,
        ],
        out_specs=pl.BlockSpec((seq, ATT_HEADS * LANES), lambda b, h: (b, h)),
        out_shape=jax.ShapeDtypeStruct((batch * seq, heads * LANES), _BF16),
        scratch_shapes=(
            [pltpu.VMEM((ATT_T, ATT_T), _F32)] * ATT_SCORE_BUFS
            + [pltpu.VMEM((ATT_T, LANES), _F32), pltpu.VMEM((ATT_T, 2 * LANES), _F32)] * 2),
        compiler_params=pltpu.CompilerParams(
            dimension_semantics=("arbitrary", "arbitrary"),
            vmem_limit_bytes=ATTENTION_VMEM),
        name=name,
    )(q, k, kx, v)


def _out_proj_kernel(om_ref, of_ref, gm_ref, gf_ref, w_ref, x_ref, o_ref):
    om = om_ref[...].astype(_F32)
    of = of_ref[...].astype(_F32)
    ym = jnp.dot((om * gm_ref[...]).astype(_BF16), w_ref[0:HEAD_WIDTH, :], preferred_element_type=_F32)
    yf = jnp.dot((of * gf_ref[...]).astype(_BF16), w_ref[HEAD_WIDTH:2 * HEAD_WIDTH, :],
                 preferred_element_type=_F32)
    o_ref[...] = x_ref[...] + _scale_rows(ym, _inv_rms(om)) + _scale_rows(yf, _inv_rms(of))


def _out_proj(o_mla, o_fox, gm, gf, w_out, x2d):
    T = x2d.shape[0]
    tm = OUT_TM
    return pl.pallas_call(
        _out_proj_kernel,
        grid=(T // tm,),
        in_specs=[
            pl.BlockSpec((tm, HEAD_WIDTH), lambda i: (i, 0)),
            pl.BlockSpec((tm, HEAD_WIDTH), lambda i: (i, 0)),
            pl.BlockSpec((1, HEAD_WIDTH), lambda i: (0, 0)),
            pl.BlockSpec((1, HEAD_WIDTH), lambda i: (0, 0)),
            pl.BlockSpec((2 * HEAD_WIDTH, D_MODEL), lambda i: (0, 0)),
            pl.BlockSpec((tm, D_MODEL), lambda i: (i, 0)),
        ],
        out_specs=pl.BlockSpec((tm, D_MODEL), lambda i: (i, 0)),
        out_shape=jax.ShapeDtypeStruct((T, D_MODEL), _F32),
        compiler_params=pltpu.CompilerParams(
            dimension_semantics=("arbitrary",),
            vmem_limit_bytes=OUT_PROJ_VMEM),
        name="out_proj",
    )(o_mla, o_fox, gm, gf, w_out, x2d)


def _ffn_kernel(x_ref, g_ref, wg_ref, wu_ref, wd_ref, gfin_ref, o_ref, h_scr, inv_scr):
    j = pl.program_id(1)

    def hidden_tile(h, inv):
        gate = _scale_rows(jnp.dot(h, wg_ref[...], preferred_element_type=_F32), inv)
        up = _scale_rows(jnp.dot(h, wu_ref[...], preferred_element_type=_F32), inv)
        a = (gate * (1.0 / (1.0 + jnp.exp(-gate))) * up).astype(_BF16)
        return jnp.dot(a, wd_ref[...], preferred_element_type=_F32)

    @pl.when(j == 0)
    def _():
        x = x_ref[...]
        h = (x * g_ref[...]).astype(_BF16)
        h_scr[...] = h
        inv = _inv_rms(x)
        inv_scr[...] = inv
        o_ref[...] = hidden_tile(h, inv)

    last = pl.num_programs(1) - 1

    @pl.when((j > 0) & (j < last))
    def _():
        o_ref[...] += hidden_tile(h_scr[...], inv_scr[...])

    @pl.when(j == last)
    def _():
        y = x_ref[...] + o_ref[...] + hidden_tile(h_scr[...], inv_scr[...])
        o_ref[...] = _rms_scale(y) * gfin_ref[...]


def _ffn(x1, g, w_gate, w_up, w_down, g_final):
    T = x1.shape[0]
    tm, th = FFN_TM, FFN_TH
    nh = FFN_HIDDEN // th

    def hid(i, j):
        return jnp.where(i % 2 == 0, j, nh - 1 - j)

    return pl.pallas_call(
        _ffn_kernel,
        grid=(T // tm, nh),
        in_specs=[
            pl.BlockSpec((tm, D_MODEL), lambda i, j: (i, 0)),
            pl.BlockSpec((1, D_MODEL), lambda i, j: (0, 0)),
            pl.BlockSpec((D_MODEL, th), lambda i, j: (0, hid(i, j))),
            pl.BlockSpec((D_MODEL, th), lambda i, j: (0, hid(i, j))),
            pl.BlockSpec((th, D_MODEL), lambda i, j: (hid(i, j), 0)),
            pl.BlockSpec((1, D_MODEL), lambda i, j: (0, 0)),
        ],
        out_specs=pl.BlockSpec((tm, D_MODEL), lambda i, j: (i, 0)),
        out_shape=jax.ShapeDtypeStruct((T, D_MODEL), _F32),
        scratch_shapes=[pltpu.VMEM((tm, D_MODEL), _BF16), pltpu.VMEM((tm, LANES), _F32)],
        compiler_params=pltpu.CompilerParams(
            dimension_semantics=("arbitrary", "arbitrary"),
            vmem_limit_bytes=FFN_VMEM),
        name="ffn",
    )(x1, g, w_gate, w_up, w_down, g_final)


def _prep_w_in(w):
    w = w.astype(_BF16)
    lat = Q_LORA + KV_LORA
    kr = w[:, lat:lat + QK_ROPE]
    fox0 = lat + QK_ROPE
    f = w[:, fox0 + 3 * HEAD_WIDTH:]
    pad = jnp.zeros((w.shape[0], LANES - 3 * FOX_HEADS), w.dtype)
    misc = jnp.concatenate([f, f, f, pad, w[:, :lat], kr, kr], axis=1)
    return misc, w[:, fox0:fox0 + 3 * HEAD_WIDTH]


def _prep_w_uq(w):
    w3 = w.reshape(Q_LORA, MLA_HEADS, QK_NOPE + QK_ROPE)
    w3 = jnp.concatenate([w3, w3[:, :, QK_NOPE:]], axis=2)
    return w3.reshape(Q_LORA, MLA_HEADS * QK_WIDTH).astype(_BF16)


def kernel(x, positions, g_attn_norm, w_in, b_forget, g_q_lat, w_uq, g_kv_lat, w_ukv, g_out_mla, g_out_fox, w_out, g_ffn_norm, w_gate, w_up, w_down, g_final_norm):
    B, S, D = x.shape
    T = B * S
    assert w_in.shape[0] == 1, "one layer (DEPTH == 1) is supported"
    inv_freq = ROPE_THETA ** (-jnp.arange(0, QK_ROPE, 2, dtype=_F32) / QK_ROPE)
    invf_row = jnp.tile(inv_freq, LANES // (QK_ROPE // 2)).reshape(1, LANES)
    pos_f = positions.astype(_F32).reshape(T, 1)
    x2d = x.reshape(T, D)

    b_row = jnp.concatenate(
        [b_forget[0]] * 3 + [jnp.zeros((LANES - 3 * FOX_HEADS,), _F32)]).reshape(1, LANES)
    w_misc, w_fox = _prep_w_in(w_in[0])
    misc, dec, rot, fox = _in_proj(x2d, g_attn_norm[0].reshape(1, D), w_misc, w_fox, b_row,
                                   pos_f, invf_row, S)
    mq, mkn, mkpe, mv = _mla_up(misc, rot, g_q_lat[0].reshape(1, Q_LORA),
                                g_kv_lat[0].reshape(1, KV_LORA), _prep_w_uq(w_uq[0]), w_ukv[0].astype(_BF16))

    def head_spec(width):
        return pl.BlockSpec((ATT_HEADS, S, width), lambda b, h: (h, b, 0))

    def slab_spec(slab):
        return pl.BlockSpec((None, ATT_HEADS, S, LANES), lambda b, h: (slab, h, b, 0))

    o_mla = _attention(mq, head_spec(QK_WIDTH), mkn, head_spec(LANES), mkpe, mv, head_spec(LANES),
                       B, S, MLA_HEADS, CHUNK, False, "mla_attn")
    o_fox = _attention(fox, slab_spec(0), fox, slab_spec(1), dec, fox, slab_spec(2),
                       B, S, FOX_HEADS, 1, True, "fox_attn")
    x1 = _out_proj(o_mla, o_fox, g_out_mla[0].reshape(1, HEAD_WIDTH),
                   g_out_fox[0].reshape(1, HEAD_WIDTH), w_out[0].astype(_BF16), x2d)
    out = _ffn(x1, g_ffn_norm[0].reshape(1, D), w_gate[0].astype(_BF16), w_up[0].astype(_BF16),
               w_down[0].astype(_BF16), g_final_norm.reshape(1, D))
    return out.reshape(B, S, D)
```

```python
import functools
import math

import jax
import jax.numpy as jnp
from jax import lax
from jax.experimental import pallas as pl
from jax.experimental.pallas import tpu as pltpu

D_MODEL = 2048
CHUNK = 64
EPS = 1e-6
ROPE_THETA = 10000.0
MLA_HEADS = 8
Q_LORA = 512
KV_LORA = 256
QK_NOPE = 128
QK_ROPE = 64
V_HEAD = 128
FOX_HEADS = 8
FOX_HEAD_DIM = 128
HEAD_WIDTH = 1024
FFN_HIDDEN = 5632

LANES = 128
QK_WIDTH = 256
LOG2E = math.log2(math.e)
MASK_VALUE = -1e30

IN_TM = 1024
IN_TN = 1024
CUM_ROWS = 256
UP_TM = 1024
ATT_T = 512
ATT_SCORE_BUFS = 3
ATT_HEADS = 1
OUT_TM = 512
FFN_TM = 1024
FFN_TH = 512

_MIB = 1024 * 1024
IN_PROJ_VMEM = 60 * _MIB
MLA_UP_VMEM = 56 * _MIB
ATTENTION_VMEM = 40 * _MIB
OUT_PROJ_VMEM = 48 * _MIB
FFN_VMEM = 63 * _MIB

_BF16 = jnp.bfloat16
_F32 = jnp.float32


def _rms_scale(x):
    return x * lax.rsqrt(jnp.mean(x * x, axis=-1, keepdims=True) + EPS)


def _inv_rms(x):
    inv = lax.rsqrt(jnp.mean(x * x, axis=-1, keepdims=True) + EPS)
    return jnp.broadcast_to(inv, (x.shape[0], LANES))


def _scale_rows(y, inv):
    return jnp.concatenate(
        [y[:, c * LANES:(c + 1) * LANES] * inv for c in range(y.shape[1] // LANES)], axis=1)


def _sincos(x):
    k = jnp.floor(x * (2.0 / math.pi) + 0.5)
    r = ((x - k * 1.5703125) - k * 4.837512969970703125e-4) - k * 7.54978995489188216e-8
    r2 = r * r
    sin_r = r + r * r2 * (-1.6666654611e-1 + r2 * (8.3321608736e-3 + r2 * -1.9515295891e-4))
    cos_r = 1.0 - 0.5 * r2 + r2 * r2 * (4.166664568298827e-2 + r2 * (-1.388731625493765e-3
                                                                    + r2 * 2.443315711809948e-5))
    half_k = jnp.floor(k * 0.5)
    odd = k - 2.0 * half_k
    sign = 1.0 - 2.0 * (half_k - 2.0 * jnp.floor(half_k * 0.5))
    return sign * (sin_r + odd * (cos_r - sin_r)), sign * (cos_r - odd * (cos_r + sin_r))


def _split3_bf16(x):
    hi = x.astype(_BF16)
    r = x - hi.astype(_F32)
    mid = r.astype(_BF16)
    lo = (r - mid.astype(_F32)).astype(_BF16)
    return hi, mid, lo


def _in_proj_kernel(x_ref, g_ref, wm_ref, wf_ref, b_ref, pos_ref, invf_ref,
                    misc_ref, dec_ref, rot_ref, qkv_ref,
                    h_scr, inv_scr, f_scr, carry_scr, *, tiles_per_seq, q_scale):
    i = pl.program_id(0)
    j = pl.program_id(1)
    tm = x_ref.shape[0]
    lane = lax.broadcasted_iota(jnp.int32, (tm, LANES), 1)

    @pl.when(j == 0)
    def _():
        x = x_ref[...]
        h = (x * g_ref[...]).astype(_BF16)
        h_scr[...] = h
        inv = _inv_rms(x)
        inv_scr[...] = inv
        acc = _scale_rows(jnp.dot(h, wm_ref[...], preferred_element_type=_F32), inv)
        misc_ref[...] = acc.astype(_BF16)
        f_scr[...] = acc[:, 0:LANES] + b_ref[...]

    def decay_columns():
        f = f_scr[...]
        log_f = jnp.minimum(f, 0.0) - jnp.log1p(jnp.exp(-jnp.abs(f)))
        hi, mid, lo = _split3_bf16(log_f)
        zero = jnp.zeros_like(hi)
        parts = jnp.where(lane < 8, hi, jnp.where(lane < 16, mid, jnp.where(lane < 24, lo, zero)))
        r = lax.broadcasted_iota(jnp.int32, (CUM_ROWS, CUM_ROWS), 0)
        c = lax.broadcasted_iota(jnp.int32, (CUM_ROWS, CUM_ROWS), 1)
        tri = jnp.where(c <= r, 1.0, 0.0).astype(_BF16)

        carry = jnp.where(i % tiles_per_seq == 0, 0.0, carry_scr[...])
        chunks = []
        for ch in range(tm // CUM_ROWS):
            y = jnp.dot(tri, parts[ch * CUM_ROWS:(ch + 1) * CUM_ROWS], preferred_element_type=_F32)
            y = (y + pltpu.roll(y, 8, 1) + pltpu.roll(y, 16, 1)
                 + pltpu.roll(y, LANES - 8, 1) + pltpu.roll(y, LANES - 16, 1))
            chunks.append(y + carry)
            carry = carry + y[CUM_ROWS - 1:CUM_ROWS, :]
        carry_scr[...] = carry
        cum = jnp.concatenate(chunks, axis=0)
        dhi, dmid, dlo = _split3_bf16(cum * (-LOG2E))
        dec_ref[...] = jnp.where(lane < 8, dhi, jnp.where(lane < 16, dmid, jnp.where(lane < 24, dlo, zero)))

    def store_heads(scale):
        acc = jnp.dot(h_scr[...], wf_ref[...], preferred_element_type=_F32)
        for h in range(FOX_HEADS):
            qkv_ref[0, h] = (acc[:, h * LANES:(h + 1) * LANES] * scale).astype(_BF16)

    slab = _fox_slab(i, j)

    @pl.when((j > 0) & (slab == 0))
    def _():
        store_heads(inv_scr[...] * q_scale)
        sin, cos = _sincos(pos_ref[...] * invf_ref[...])
        rot_ref[:, 0:LANES] = cos
        rot_ref[:, LANES:2 * LANES] = jnp.where((lane % QK_ROPE) < QK_ROPE // 2, -sin, sin)

    @pl.when((j > 0) & (slab == 1))
    def _():
        store_heads(inv_scr[...])
        decay_columns()

    @pl.when((j > 0) & (slab == 2))
    def _():
        store_heads(inv_scr[...])


def _fox_slab(i, j):
    return jnp.where(i % 2 == 0, jnp.maximum(j - 1, 0), jnp.minimum(3 - j, 2))


def _in_proj(x2d, g, w_misc, w_fox, b_row, pos_f, invf_row, seq):
    T = x2d.shape[0]
    tm = IN_TM
    grid = (T // tm, 1 + w_fox.shape[1] // IN_TN)
    kern = functools.partial(_in_proj_kernel, tiles_per_seq=seq // tm,
                             q_scale=LOG2E / math.sqrt(FOX_HEAD_DIM))
    return pl.pallas_call(
        kern,
        grid=grid,
        in_specs=[
            pl.BlockSpec((tm, D_MODEL), lambda i, j: (i, 0)),
            pl.BlockSpec((1, D_MODEL), lambda i, j: (0, 0)),
            pl.BlockSpec((D_MODEL, IN_TN), lambda i, j: (0, 0)),
            pl.BlockSpec((D_MODEL, IN_TN), lambda i, j: (0, _fox_slab(i, j))),
            pl.BlockSpec((1, LANES), lambda i, j: (0, 0)),
            pl.BlockSpec((tm, 1), lambda i, j: (i, 0)),
            pl.BlockSpec((1, LANES), lambda i, j: (0, 0)),
        ],
        out_specs=[
            pl.BlockSpec((tm, IN_TN), lambda i, j: (i, 0)),
            pl.BlockSpec((tm, LANES), lambda i, j: (i, 0)),
            pl.BlockSpec((tm, 2 * LANES), lambda i, j: (i, 0)),
            pl.BlockSpec((1, FOX_HEADS, tm, LANES), lambda i, j: (_fox_slab(i, j), 0, i, 0)),
        ],
        out_shape=[
            jax.ShapeDtypeStruct((T, IN_TN), _BF16),
            jax.ShapeDtypeStruct((T, LANES), _BF16),
            jax.ShapeDtypeStruct((T, 2 * LANES), _F32),
            jax.ShapeDtypeStruct((3, FOX_HEADS, T, LANES), _BF16),
        ],
        scratch_shapes=[
            pltpu.VMEM((tm, D_MODEL), _BF16),
            pltpu.VMEM((tm, LANES), _F32),
            pltpu.VMEM((tm, LANES), _F32),
            pltpu.VMEM((1, LANES), _F32),
        ],
        compiler_params=pltpu.CompilerParams(
            dimension_semantics=("arbitrary", "arbitrary"),
            vmem_limit_bytes=IN_PROJ_VMEM),
        name="in_proj",
    )(x2d, g, w_misc, w_fox, b_row, pos_f, invf_row)


def _mla_up_kernel(misc_ref, rot_ref, gq_ref, gkv_ref, wuq_ref, wukv_ref,
                   q_ref, kn_ref, kpe_ref, v_ref, *, q_scale):
    tm = misc_ref.shape[0]
    q_lat = misc_ref[:, LANES:LANES + Q_LORA].astype(_F32)
    kv_lat = misc_ref[:, LANES + Q_LORA:LANES + Q_LORA + KV_LORA].astype(_F32)
    k_rope = misc_ref[:, LANES + Q_LORA + KV_LORA:IN_TN].astype(_F32)

    qn = (_rms_scale(q_lat) * gq_ref[...]).astype(_BF16)
    kvn = (_rms_scale(kv_lat) * gkv_ref[...]).astype(_BF16)
    q = jnp.dot(qn, wuq_ref[...], preferred_element_type=_F32)
    kv = jnp.dot(kvn, wukv_ref[...], preferred_element_type=_F32)

    cos = rot_ref[:, 0:LANES]
    sin_signed = rot_ref[:, LANES:2 * LANES]
    lane = lax.broadcasted_iota(jnp.int32, (tm, LANES), 1)

    def rope(p):
        return p * cos + pltpu.roll(p, QK_ROPE // 2, 1) * sin_signed

    kpe_ref[...] = jnp.where(lane < QK_ROPE, rope(k_rope), 0.0).astype(_BF16)
    for h in range(MLA_HEADS):
        base = h * QK_WIDTH
        q_ref[h, :, 0:LANES] = (q[:, base:base + LANES] * q_scale).astype(_BF16)
        q_ref[h, :, LANES:QK_WIDTH] = (rope(q[:, base + LANES:base + QK_WIDTH]) * q_scale).astype(_BF16)
        kn_ref[h] = kv[:, base:base + LANES].astype(_BF16)
        v_ref[h] = kv[:, base + LANES:base + QK_WIDTH].astype(_BF16)


def _mla_up(misc, rot, gq, gkv, w_uq_r, w_ukv):
    T = misc.shape[0]
    tm = UP_TM
    kern = functools.partial(_mla_up_kernel, q_scale=LOG2E / math.sqrt(QK_NOPE + QK_ROPE))
    return pl.pallas_call(
        kern,
        grid=(T // tm,),
        in_specs=[
            pl.BlockSpec((tm, IN_TN), lambda i: (i, 0)),
            pl.BlockSpec((tm, 2 * LANES), lambda i: (i, 0)),
            pl.BlockSpec((1, Q_LORA), lambda i: (0, 0)),
            pl.BlockSpec((1, KV_LORA), lambda i: (0, 0)),
            pl.BlockSpec((Q_LORA, MLA_HEADS * QK_WIDTH), lambda i: (0, 0)),
            pl.BlockSpec((KV_LORA, MLA_HEADS * QK_WIDTH), lambda i: (0, 0)),
        ],
        out_specs=[
            pl.BlockSpec((MLA_HEADS, tm, QK_WIDTH), lambda i: (0, i, 0)),
            pl.BlockSpec((MLA_HEADS, tm, LANES), lambda i: (0, i, 0)),
            pl.BlockSpec((tm, LANES), lambda i: (i, 0)),
            pl.BlockSpec((MLA_HEADS, tm, LANES), lambda i: (0, i, 0)),
        ],
        out_shape=[
            jax.ShapeDtypeStruct((MLA_HEADS, T, QK_WIDTH), _BF16),
            jax.ShapeDtypeStruct((MLA_HEADS, T, LANES), _BF16),
            jax.ShapeDtypeStruct((T, LANES), _BF16),
            jax.ShapeDtypeStruct((MLA_HEADS, T, LANES), _BF16),
        ],
        compiler_params=pltpu.CompilerParams(
            dimension_semantics=("arbitrary",),
            vmem_limit_bytes=MLA_UP_VMEM),
        name="mla_up",
    )(misc, rot, gq, gkv, w_uq_r, w_ukv)


def _attn_kernel(q_ref, k_ref, kx_ref, v_ref, o_ref, *scratch, causal_unit, one_hot_q):
    s_scr, state_scr = scratch[:ATT_SCORE_BUFS], scratch[ATT_SCORE_BUFS:]
    seq = k_ref.shape[1]
    t = ATT_T
    nt = seq // t
    row = lax.broadcasted_iota(jnp.int32, (t, t), 0)
    col = lax.broadcasted_iota(jnp.int32, (t, t), 1)
    if causal_unit == 1:
        allowed = col <= row
    else:
        allowed = (col // causal_unit) <= (row // causal_unit)
    ones = jnp.ones((t, LANES), _BF16)
    qx = []
    if one_hot_q:
        lane = lax.broadcasted_iota(jnp.int32, (t, LANES), 1)
        for hh in range(ATT_HEADS):
            h = ATT_HEADS * pl.program_id(1) + hh
            pick = (lane == h) | (lane == h + FOX_HEADS) | (lane == h + 2 * FOX_HEADS)
            qx.append(jnp.where(pick, 1.0, 0.0).astype(_BF16))

    def q_rows(hh, qs):
        q = q_ref[hh, qs:qs + t, :]
        return jnp.concatenate([q, qx[hh]], axis=1) if one_hot_q else q

    def scores(q, hh, ks, kw):
        k = jnp.concatenate([k_ref[hh, ks:ks + kw, :], kx_ref[ks:ks + kw, :]], axis=1)
        return lax.dot_general(q, k, (((1,), (1,)), ((), ())), preferred_element_type=_F32)

    def update(s, hh, rows, ks, m_scr, acc_scr):
        kw = s.shape[1]
        m = m_scr[rows, :]
        m_new = jnp.maximum(m, jnp.max(s, axis=1, keepdims=True))
        alpha = jnp.exp2(m - m_new)
        m_scr[rows, :] = m_new
        p = jnp.concatenate(
            [jnp.exp2(s[:, c * LANES:(c + 1) * LANES] - m_new) for c in range(kw // LANES)], axis=1)
        v = jnp.concatenate([v_ref[hh, ks:ks + kw, :], ones[0:kw]], axis=1)
        pv = jnp.dot(p.astype(_BF16), v, preferred_element_type=_F32)
        for c in range(2):
            sl = slice(c * LANES, (c + 1) * LANES)
            acc_scr[rows, sl] = alpha * acc_scr[rows, sl] + pv[:, sl]

    tiles = [(hh, i, j) for hh in range(ATT_HEADS) for i in range(nt) for j in range(i + 1)]
    half = t // 2
    top, bottom, full = slice(0, half), slice(half, t), slice(0, t)

    def issue(n):
        hh, i, j = tiles[n]
        buf = s_scr[n % len(s_scr)]
        q = q_rows(hh, i * t)
        if j == i:
            buf[top, 0:half] = scores(q[top], hh, j * t, half)
            buf[bottom, :] = scores(q[bottom], hh, j * t, t)
        else:
            buf[...] = scores(q, hh, j * t, t)

    issue(0)
    for n, (hh, i, j) in enumerate(tiles):
        parity = (hh * nt + i) % 2
        m_scr, acc_scr = state_scr[2 * parity], state_scr[2 * parity + 1]
        if j == 0:
            m_scr[...] = jnp.full(m_scr.shape, MASK_VALUE, _F32)
            acc_scr[...] = jnp.zeros_like(acc_scr)
        if n + 1 < len(tiles):
            issue(n + 1)
        buf = s_scr[n % len(s_scr)]
        if j == i:
            update(jnp.where(allowed[top, 0:half], buf[top, 0:half], MASK_VALUE), hh, top, j * t, m_scr, acc_scr)
            update(jnp.where(allowed[bottom, :], buf[bottom, :], MASK_VALUE), hh, bottom, j * t, m_scr, acc_scr)
        else:
            update(buf[...], hh, full, j * t, m_scr, acc_scr)
        if j == i:
            o_ref[i * t:(i + 1) * t, hh * LANES:(hh + 1) * LANES] = (
                acc_scr[:, 0:LANES] / acc_scr[:, LANES:2 * LANES]).astype(o_ref.dtype)


def _attention(q, q_spec, k, k_spec, kx, v, v_spec, batch, seq, heads, causal_unit, one_hot_q, name):
    kern = functools.partial(_attn_kernel, causal_unit=causal_unit, one_hot_q=one_hot_q)
    return pl.pallas_call(
        kern,
        grid=(batch, heads // ATT_HEADS),
        in_specs=[
            q_spec,
            k_spec,
            pl.BlockSpec((seq, LANES), lambda b, h: (b, 0)),
            v_spec,
        ],
        out_specs=pl.BlockSpec((seq, ATT_HEADS * LANES), lambda b, h: (b, h)),
        out_shape=jax.ShapeDtypeStruct((batch * seq, heads * LANES), _BF16),
        scratch_shapes=(
            [pltpu.VMEM((ATT_T, ATT_T), _F32)] * ATT_SCORE_BUFS
            + [pltpu.VMEM((ATT_T, LANES), _F32), pltpu.VMEM((ATT_T, 2 * LANES), _F32)] * 2),
        compiler_params=pltpu.CompilerParams(
            dimension_semantics=("arbitrary", "arbitrary"),
            vmem_limit_bytes=ATTENTION_VMEM),
        name=name,
    )(q, k, kx, v)


def _out_proj_kernel(om_ref, of_ref, gm_ref, gf_ref, w_ref, x_ref, o_ref):
    om = om_ref[...].astype(_F32)
    of = of_ref[...].astype(_F32)
    ym = jnp.dot((om * gm_ref[...]).astype(_BF16), w_ref[0:HEAD_WIDTH, :], preferred_element_type=_F32)
    yf = jnp.dot((of * gf_ref[...]).astype(_BF16), w_ref[HEAD_WIDTH:2 * HEAD_WIDTH, :],
                 preferred_element_type=_F32)
    o_ref[...] = x_ref[...] + _scale_rows(ym, _inv_rms(om)) + _scale_rows(yf, _inv_rms(of))


def _out_proj(o_mla, o_fox, gm, gf, w_out, x2d):
    T = x2d.shape[0]
    tm = OUT_TM
    return pl.pallas_call(
        _out_proj_kernel,
        grid=(T // tm,),
        in_specs=[
            pl.BlockSpec((tm, HEAD_WIDTH), lambda i: (i, 0)),
            pl.BlockSpec((tm, HEAD_WIDTH), lambda i: (i, 0)),
            pl.BlockSpec((1, HEAD_WIDTH), lambda i: (0, 0)),
            pl.BlockSpec((1, HEAD_WIDTH), lambda i: (0, 0)),
            pl.BlockSpec((2 * HEAD_WIDTH, D_MODEL), lambda i: (0, 0)),
            pl.BlockSpec((tm, D_MODEL), lambda i: (i, 0)),
        ],
        out_specs=pl.BlockSpec((tm, D_MODEL), lambda i: (i, 0)),
        out_shape=jax.ShapeDtypeStruct((T, D_MODEL), _F32),
        compiler_params=pltpu.CompilerParams(
            dimension_semantics=("arbitrary",),
            vmem_limit_bytes=OUT_PROJ_VMEM),
        name="out_proj",
    )(o_mla, o_fox, gm, gf, w_out, x2d)


def _ffn_kernel(x_ref, g_ref, wg_ref, wu_ref, wd_ref, gfin_ref, o_ref, h_scr, inv_scr):
    j = pl.program_id(1)

    def hidden_tile(h, inv):
        gate = _scale_rows(jnp.dot(h, wg_ref[...], preferred_element_type=_F32), inv)
        up = _scale_rows(jnp.dot(h, wu_ref[...], preferred_element_type=_F32), inv)
        a = (gate * (1.0 / (1.0 + jnp.exp(-gate))) * up).astype(_BF16)
        return jnp.dot(a, wd_ref[...], preferred_element_type=_F32)

    @pl.when(j == 0)
    def _():
        x = x_ref[...]
        h = (x * g_ref[...]).astype(_BF16)
        h_scr[...] = h
        inv = _inv_rms(x)
        inv_scr[...] = inv
        o_ref[...] = hidden_tile(h, inv)

    last = pl.num_programs(1) - 1

    @pl.when((j > 0) & (j < last))
    def _():
        o_ref[...] += hidden_tile(h_scr[...], inv_scr[...])

    @pl.when(j == last)
    def _():
        y = x_ref[...] + o_ref[...] + hidden_tile(h_scr[...], inv_scr[...])
        o_ref[...] = _rms_scale(y) * gfin_ref[...]


def _ffn(x1, g, w_gate, w_up, w_down, g_final):
    T = x1.shape[0]
    tm, th = FFN_TM, FFN_TH
    nh = FFN_HIDDEN // th

    def hid(i, j):
        return jnp.where(i % 2 == 0, j, nh - 1 - j)

    return pl.pallas_call(
        _ffn_kernel,
        grid=(T // tm, nh),
        in_specs=[
            pl.BlockSpec((tm, D_MODEL), lambda i, j: (i, 0)),
            pl.BlockSpec((1, D_MODEL), lambda i, j: (0, 0)),
            pl.BlockSpec((D_MODEL, th), lambda i, j: (0, hid(i, j))),
            pl.BlockSpec((D_MODEL, th), lambda i, j: (0, hid(i, j))),
            pl.BlockSpec((th, D_MODEL), lambda i, j: (hid(i, j), 0)),
            pl.BlockSpec((1, D_MODEL), lambda i, j: (0, 0)),
        ],
        out_specs=pl.BlockSpec((tm, D_MODEL), lambda i, j: (i, 0)),
        out_shape=jax.ShapeDtypeStruct((T, D_MODEL), _F32),
        scratch_shapes=[pltpu.VMEM((tm, D_MODEL), _BF16), pltpu.VMEM((tm, LANES), _F32)],
        compiler_params=pltpu.CompilerParams(
            dimension_semantics=("arbitrary", "arbitrary"),
            vmem_limit_bytes=FFN_VMEM),
        name="ffn",
    )(x1, g, w_gate, w_up, w_down, g_final)


def _prep_w_in(w):
    w = w.astype(_BF16)
    lat = Q_LORA + KV_LORA
    kr = w[:, lat:lat + QK_ROPE]
    fox0 = lat + QK_ROPE
    f = w[:, fox0 + 3 * HEAD_WIDTH:]
    pad = jnp.zeros((w.shape[0], LANES - 3 * FOX_HEADS), w.dtype)
    misc = jnp.concatenate([f, f, f, pad, w[:, :lat], kr, kr], axis=1)
    return misc, w[:, fox0:fox0 + 3 * HEAD_WIDTH]


def _prep_w_uq(w):
    w3 = w.reshape(Q_LORA, MLA_HEADS, QK_NOPE + QK_ROPE)
    w3 = jnp.concatenate([w3, w3[:, :, QK_NOPE:]], axis=2)
    return w3.reshape(Q_LORA, MLA_HEADS * QK_WIDTH).astype(_BF16)


def kernel(x, positions, g_attn_norm, w_in, b_forget, g_q_lat, w_uq, g_kv_lat, w_ukv, g_out_mla, g_out_fox, w_out, g_ffn_norm, w_gate, w_up, w_down, g_final_norm):
    B, S, D = x.shape
    T = B * S
    assert w_in.shape[0] == 1, "one layer (DEPTH == 1) is supported"
    inv_freq = ROPE_THETA ** (-jnp.arange(0, QK_ROPE, 2, dtype=_F32) / QK_ROPE)
    invf_row = jnp.tile(inv_freq, LANES // (QK_ROPE // 2)).reshape(1, LANES)
    pos_f = positions.astype(_F32).reshape(T, 1)
    x2d = x.reshape(T, D)

    b_row = jnp.concatenate(
        [b_forget[0]] * 3 + [jnp.zeros((LANES - 3 * FOX_HEADS,), _F32)]).reshape(1, LANES)
    w_misc, w_fox = _prep_w_in(w_in[0])
    misc, dec, rot, fox = _in_proj(x2d, g_attn_norm[0].reshape(1, D), w_misc, w_fox, b_row,
                                   pos_f, invf_row, S)
    mq, mkn, mkpe, mv = _mla_up(misc, rot, g_q_lat[0].reshape(1, Q_LORA),
                                g_kv_lat[0].reshape(1, KV_LORA), _prep_w_uq(w_uq[0]), w_ukv[0].astype(_BF16))

    def head_spec(width):
        return pl.BlockSpec((ATT_HEADS, S, width), lambda b, h: (h, b, 0))

    def slab_spec(slab):
        return pl.BlockSpec((None, ATT_HEADS, S, LANES), lambda b, h: (slab, h, b, 0))

    o_mla = _attention(mq, head_spec(QK_WIDTH), mkn, head_spec(LANES), mkpe, mv, head_spec(LANES),
                       B, S, MLA_HEADS, CHUNK, False, "mla_attn")
    o_fox = _attention(fox, slab_spec(0), fox, slab_spec(1), dec, fox, slab_spec(2),
                       B, S, FOX_HEADS, 1, True, "fox_attn")
    x1 = _out_proj(o_mla, o_fox, g_out_mla[0].reshape(1, HEAD_WIDTH),
                   g_out_fox[0].reshape(1, HEAD_WIDTH), w_out[0].astype(_BF16), x2d)
    out = _ffn(x1, g_ffn_norm[0].reshape(1, D), w_gate[0].astype(_BF16), w_up[0].astype(_BF16),
               w_down[0].astype(_BF16), g_final_norm.reshape(1, D))
    return out.reshape(B, S, D)
```

```python
import functools
import math

import jax
import jax.numpy as jnp
from jax import lax
from jax.experimental import pallas as pl
from jax.experimental.pallas import tpu as pltpu

D_MODEL = 2048
CHUNK = 64
EPS = 1e-6
ROPE_THETA = 10000.0
MLA_HEADS = 8
Q_LORA = 512
KV_LORA = 256
QK_NOPE = 128
QK_ROPE = 64
V_HEAD = 128
FOX_HEADS = 8
FOX_HEAD_DIM = 128
HEAD_WIDTH = 1024
FFN_HIDDEN = 5632

LANES = 128
QK_WIDTH = 256
LOG2E = math.log2(math.e)
MASK_VALUE = -1e30

IN_TM = 1024
IN_TN = 1024
CUM_ROWS = 256
UP_TM = 1024
ATT_T = 512
ATT_SCORE_BUFS = 3
ATT_HEADS = 1
OUT_TM = 512
FFN_TM = 1024
FFN_TH = 512

_MIB = 1024 * 1024
IN_PROJ_VMEM = 60 * _MIB
MLA_UP_VMEM = 56 * _MIB
ATTENTION_VMEM = 40 * _MIB
OUT_PROJ_VMEM = 48 * _MIB
FFN_VMEM = 63 * _MIB

_BF16 = jnp.bfloat16
_F32 = jnp.float32


def _rms_scale(x):
    return x * lax.rsqrt(jnp.mean(x * x, axis=-1, keepdims=True) + EPS)


def _inv_rms(x):
    inv = lax.rsqrt(jnp.mean(x * x, axis=-1, keepdims=True) + EPS)
    return jnp.broadcast_to(inv, (x.shape[0], LANES))


def _scale_rows(y, inv):
    return jnp.concatenate(
        [y[:, c * LANES:(c + 1) * LANES] * inv for c in range(y.shape[1] // LANES)], axis=1)


def _sincos(x):
    k = jnp.floor(x * (2.0 / math.pi) + 0.5)
    r = ((x - k * 1.5703125) - k * 4.837512969970703125e-4) - k * 7.54978995489188216e-8
    r2 = r * r
    sin_r = r + r * r2 * (-1.6666654611e-1 + r2 * (8.3321608736e-3 + r2 * -1.9515295891e-4))
    cos_r = 1.0 - 0.5 * r2 + r2 * r2 * (4.166664568298827e-2 + r2 * (-1.388731625493765e-3
                                                                    + r2 * 2.443315711809948e-5))
    half_k = jnp.floor(k * 0.5)
    odd = k - 2.0 * half_k
    sign = 1.0 - 2.0 * (half_k - 2.0 * jnp.floor(half_k * 0.5))
    return sign * (sin_r + odd * (cos_r - sin_r)), sign * (cos_r - odd * (cos_r + sin_r))


def _split3_bf16(x):
    hi = x.astype(_BF16)
    r = x - hi.astype(_F32)
    mid = r.astype(_BF16)
    lo = (r - mid.astype(_F32)).astype(_BF16)
    return hi, mid, lo


def _in_proj_kernel(x_ref, g_ref, wm_ref, wf_ref, b_ref, pos_ref, invf_ref,
                    misc_ref, dec_ref, rot_ref, qkv_ref,
                    h_scr, inv_scr, f_scr, carry_scr, *, tiles_per_seq, q_scale):
    i = pl.program_id(0)
    j = pl.program_id(1)
    tm = x_ref.shape[0]
    lane = lax.broadcasted_iota(jnp.int32, (tm, LANES), 1)

    @pl.when(j == 0)
    def _():
        x = x_ref[...]
        h = (x * g_ref[...]).astype(_BF16)
        h_scr[...] = h
        inv = _inv_rms(x)
        inv_scr[...] = inv
        acc = _scale_rows(jnp.dot(h, wm_ref[...], preferred_element_type=_F32), inv)
        misc_ref[...] = acc.astype(_BF16)
        f_scr[...] = acc[:, 0:LANES] + b_ref[...]

    def decay_columns():
        f = f_scr[...]
        log_f = jnp.minimum(f, 0.0) - jnp.log1p(jnp.exp(-jnp.abs(f)))
        hi, mid, lo = _split3_bf16(log_f)
        zero = jnp.zeros_like(hi)
        parts = jnp.where(lane < 8, hi, jnp.where(lane < 16, mid, jnp.where(lane < 24, lo, zero)))
        r = lax.broadcasted_iota(jnp.int32, (CUM_ROWS, CUM_ROWS), 0)
        c = lax.broadcasted_iota(jnp.int32, (CUM_ROWS, CUM_ROWS), 1)
        tri = jnp.where(c <= r, 1.0, 0.0).astype(_BF16)

        carry = jnp.where(i % tiles_per_seq == 0, 0.0, carry_scr[...])
        chunks = []
        for ch in range(tm // CUM_ROWS):
            y = jnp.dot(tri, parts[ch * CUM_ROWS:(ch + 1) * CUM_ROWS], preferred_element_type=_F32)
            y = (y + pltpu.roll(y, 8, 1) + pltpu.roll(y, 16, 1)
                 + pltpu.roll(y, LANES - 8, 1) + pltpu.roll(y, LANES - 16, 1))
            chunks.append(y + carry)
            carry = carry + y[CUM_ROWS - 1:CUM_ROWS, :]
        carry_scr[...] = carry
        cum = jnp.concatenate(chunks, axis=0)
        dhi, dmid, dlo = _split3_bf16(cum * (-LOG2E))
        dec_ref[...] = jnp.where(lane < 8, dhi, jnp.where(lane < 16, dmid, jnp.where(lane < 24, dlo, zero)))

    def store_heads(scale):
        acc = jnp.dot(h_scr[...], wf_ref[...], preferred_element_type=_F32)
        for h in range(FOX_HEADS):
            qkv_ref[0, h] = (acc[:, h * LANES:(h + 1) * LANES] * scale).astype(_BF16)

    slab = _fox_slab(i, j)

    @pl.when((j > 0) & (slab == 0))
    def _():
        store_heads(inv_scr[...] * q_scale)
        sin, cos = _sincos(pos_ref[...] * invf_ref[...])
        rot_ref[:, 0:LANES] = cos
        rot_ref[:, LANES:2 * LANES] = jnp.where((lane % QK_ROPE) < QK_ROPE // 2, -sin, sin)

    @pl.when((j > 0) & (slab == 1))
    def _():
        store_heads(inv_scr[...])
        decay_columns()

    @pl.when((j > 0) & (slab == 2))
    def _():
        store_heads(inv_scr[...])


def _fox_slab(i, j):
    return jnp.where(i % 2 == 0, jnp.maximum(j - 1, 0), jnp.minimum(3 - j, 2))


def _in_proj(x2d, g, w_misc, w_fox, b_row, pos_f, invf_row, seq):
    T = x2d.shape[0]
    tm = IN_TM
    grid = (T // tm, 1 + w_fox.shape[1] // IN_TN)
    kern = functools.partial(_in_proj_kernel, tiles_per_seq=seq // tm,
                             q_scale=LOG2E / math.sqrt(FOX_HEAD_DIM))
    return pl.pallas_call(
        kern,
        grid=grid,
        in_specs=[
            pl.BlockSpec((tm, D_MODEL), lambda i, j: (i, 0)),
            pl.BlockSpec((1, D_MODEL), lambda i, j: (0, 0)),
            pl.BlockSpec((D_MODEL, IN_TN), lambda i, j: (0, 0)),
            pl.BlockSpec((D_MODEL, IN_TN), lambda i, j: (0, _fox_slab(i, j))),
            pl.BlockSpec((1, LANES), lambda i, j: (0, 0)),
            pl.BlockSpec((tm, 1), lambda i, j: (i, 0)),
            pl.BlockSpec((1, LANES), lambda i, j: (0, 0)),
        ],
        out_specs=[
            pl.BlockSpec((tm, IN_TN), lambda i, j: (i, 0)),
            pl.BlockSpec((tm, LANES), lambda i, j: (i, 0)),
            pl.BlockSpec((tm, 2 * LANES), lambda i, j: (i, 0)),
            pl.BlockSpec((1, FOX_HEADS, tm, LANES), lambda i, j: (_fox_slab(i, j), 0, i, 0)),
        ],
        out_shape=[
            jax.ShapeDtypeStruct((T, IN_TN), _BF16),
            jax.ShapeDtypeStruct((T, LANES), _BF16),
            jax.ShapeDtypeStruct((T, 2 * LANES), _F32),
            jax.ShapeDtypeStruct((3, FOX_HEADS, T, LANES), _BF16),
        ],
        scratch_shapes=[
            pltpu.VMEM((tm, D_MODEL), _BF16),
            pltpu.VMEM((tm, LANES), _F32),
            pltpu.VMEM((tm, LANES), _F32),
            pltpu.VMEM((1, LANES), _F32),
        ],
        compiler_params=pltpu.CompilerParams(
            dimension_semantics=("arbitrary", "arbitrary"),
            vmem_limit_bytes=IN_PROJ_VMEM),
        name="in_proj",
    )(x2d, g, w_misc, w_fox, b_row, pos_f, invf_row)


def _mla_up_kernel(misc_ref, rot_ref, gq_ref, gkv_ref, wuq_ref, wukv_ref,
                   q_ref, kn_ref, kpe_ref, v_ref, *, q_scale):
    tm = misc_ref.shape[0]
    q_lat = misc_ref[:, LANES:LANES + Q_LORA].astype(_F32)
    kv_lat = misc_ref[:, LANES + Q_LORA:LANES + Q_LORA + KV_LORA].astype(_F32)
    k_rope = misc_ref[:, LANES + Q_LORA + KV_LORA:IN_TN].astype(_F32)

    qn = (_rms_scale(q_lat) * gq_ref[...]).astype(_BF16)
    kvn = (_rms_scale(kv_lat) * gkv_ref[...]).astype(_BF16)
    q = jnp.dot(qn, wuq_ref[...], preferred_element_type=_F32)
    kv = jnp.dot(kvn, wukv_ref[...], preferred_element_type=_F32)

    cos = rot_ref[:, 0:LANES]
    sin_signed = rot_ref[:, LANES:2 * LANES]
    lane = lax.broadcasted_iota(jnp.int32, (tm, LANES), 1)

    def rope(p):
        return p * cos + pltpu.roll(p, QK_ROPE // 2, 1) * sin_signed

    kpe_ref[...] = jnp.where(lane < QK_ROPE, rope(k_rope), 0.0).astype(_BF16)
    for h in range(MLA_HEADS):
        base = h * QK_WIDTH
        q_ref[h, :, 0:LANES] = (q[:, base:base + LANES] * q_scale).astype(_BF16)
        q_ref[h, :, LANES:QK_WIDTH] = (rope(q[:, base + LANES:base + QK_WIDTH]) * q_scale).astype(_BF16)
        kn_ref[h] = kv[:, base:base + LANES].astype(_BF16)
        v_ref[h] = kv[:, base + LANES:base + QK_WIDTH].astype(_BF16)


def _mla_up(misc, rot, gq, gkv, w_uq_r, w_ukv):
    T = misc.shape[0]
    tm = UP_TM
    kern = functools.partial(_mla_up_kernel, q_scale=LOG2E / math.sqrt(QK_NOPE + QK_ROPE))
    return pl.pallas_call(
        kern,
        grid=(T // tm,),
        in_specs=[
            pl.BlockSpec((tm, IN_TN), lambda i: (i, 0)),
            pl.BlockSpec((tm, 2 * LANES), lambda i: (i, 0)),
            pl.BlockSpec((1, Q_LORA), lambda i: (0, 0)),
            pl.BlockSpec((1, KV_LORA), lambda i: (0, 0)),
            pl.BlockSpec((Q_LORA, MLA_HEADS * QK_WIDTH), lambda i: (0, 0)),
            pl.BlockSpec((KV_LORA, MLA_HEADS * QK_WIDTH), lambda i: (0, 0)),
        ],
        out_specs=[
            pl.BlockSpec((MLA_HEADS, tm, QK_WIDTH), lambda i: (0, i, 0)),
            pl.BlockSpec((MLA_HEADS, tm, LANES), lambda i: (0, i, 0)),
            pl.BlockSpec((tm, LANES), lambda i: (i, 0)),
            pl.BlockSpec((MLA_HEADS, tm, LANES), lambda i: (0, i, 0)),
        ],
        out_shape=[
            jax.ShapeDtypeStruct((MLA_HEADS, T, QK_WIDTH), _BF16),
            jax.ShapeDtypeStruct((MLA_HEADS, T, LANES), _BF16),
            jax.ShapeDtypeStruct((T, LANES), _BF16),
            jax.ShapeDtypeStruct((MLA_HEADS, T, LANES), _BF16),
        ],
        compiler_params=pltpu.CompilerParams(
            dimension_semantics=("arbitrary",),
            vmem_limit_bytes=MLA_UP_VMEM),
        name="mla_up",
    )(misc, rot, gq, gkv, w_uq_r, w_ukv)


def _attn_kernel(q_ref, k_ref, kx_ref, v_ref, o_ref, *scratch, causal_unit, one_hot_q):
    s_scr, state_scr = scratch[:ATT_SCORE_BUFS], scratch[ATT_SCORE_BUFS:]
    seq = k_ref.shape[1]
    t = ATT_T
    nt = seq // t
    row = lax.broadcasted_iota(jnp.int32, (t, t), 0)
    col = lax.broadcasted_iota(jnp.int32, (t, t), 1)
    if causal_unit == 1:
        allowed = col <= row
    else:
        allowed = (col // causal_unit) <= (row // causal_unit)
    ones = jnp.ones((t, LANES), _BF16)
    qx = []
    if one_hot_q:
        lane = lax.broadcasted_iota(jnp.int32, (t, LANES), 1)
        for hh in range(ATT_HEADS):
            h = ATT_HEADS * pl.program_id(1) + hh
            pick = (lane == h) | (lane == h + FOX_HEADS) | (lane == h + 2 * FOX_HEADS)
            qx.append(jnp.where(pick, 1.0, 0.0).astype(_BF16))

    def q_rows(hh, qs):
        q = q_ref[hh, qs:qs + t, :]
        return jnp.concatenate([q, qx[hh]], axis=1) if one_hot_q else q

    def scores(q, hh, ks, kw):
        k = jnp.concatenate([k_ref[hh, ks:ks + kw, :], kx_ref[ks:ks + kw, :]], axis=1)
        return lax.dot_general(q, k, (((1,), (1,)), ((), ())), preferred_element_type=_F32)

    def update(s, hh, rows, ks, m_scr, acc_scr):
        kw = s.shape[1]
        m = m_scr[rows, :]
        m_new = jnp.maximum(m, jnp.max(s, axis=1, keepdims=True))
        alpha = jnp.exp2(m - m_new)
        m_scr[rows, :] = m_new
        p = jnp.concatenate(
            [jnp.exp2(s[:, c * LANES:(c + 1) * LANES] - m_new) for c in range(kw // LANES)], axis=1)
        v = jnp.concatenate([v_ref[hh, ks:ks + kw, :], ones[0:kw]], axis=1)
        pv = jnp.dot(p.astype(_BF16), v, preferred_element_type=_F32)
        for c in range(2):
            sl = slice(c * LANES, (c + 1) * LANES)
            acc_scr[rows, sl] = alpha * acc_scr[rows, sl] + pv[:, sl]

    tiles = [(hh, i, j) for hh in range(ATT_HEADS) for i in range(nt) for j in range(i + 1)]
    half = t // 2
    top, bottom, full = slice(0, half), slice(half, t), slice(0, t)

    def issue(n):
        hh, i, j = tiles[n]
        buf = s_scr[n % len(s_scr)]
        q = q_rows(hh, i * t)
        if j == i:
            buf[top, 0:half] = scores(q[top], hh, j * t, half)
            buf[bottom, :] = scores(q[bottom], hh, j * t, t)
        else:
            buf[...] = scores(q, hh, j * t, t)

    ahead = len(s_scr) - 1
    for n in range(min(ahead, len(tiles))):
        issue(n)
    for n, (hh, i, j) in enumerate(tiles):
        parity = (hh * nt + i) % 2
        m_scr, acc_scr = state_scr[2 * parity], state_scr[2 * parity + 1]
        if j == 0:
            m_scr[...] = jnp.full(m_scr.shape, MASK_VALUE, _F32)
            acc_scr[...] = jnp.zeros_like(acc_scr)
        if n + ahead < len(tiles):
            issue(n + ahead)
        buf = s_scr[n % len(s_scr)]
        if j == i:
            update(jnp.where(allowed[top, 0:half], buf[top, 0:half], MASK_VALUE), hh, top, j * t, m_scr, acc_scr)
            update(jnp.where(allowed[bottom, :], buf[bottom, :], MASK_VALUE), hh, bottom, j * t, m_scr, acc_scr)
        else:
            update(buf[...], hh, full, j * t, m_scr, acc_scr)
        if j == i:
            o_ref[i * t:(i + 1) * t, hh * LANES:(hh + 1) * LANES] = (
                acc_scr[:, 0:LANES] / acc_scr[:, LANES:2 * LANES]).astype(o_ref.dtype)


def _attention(q, q_spec, k, k_spec, kx, v, v_spec, batch, seq, heads, causal_unit, one_hot_q, name):
    kern = functools.partial(_attn_kernel, causal_unit=causal_unit, one_hot_q=one_hot_q)
    return pl.pallas_call(
        kern,
        grid=(batch, heads // ATT_HEADS),
        in_specs=[
            q_spec,
            k_spec,
            pl.BlockSpec((seq, LANES), lambda b, h: (b, 0)),
            v_spec,
        ],
        out_specs=pl.BlockSpec((seq, ATT_HEADS * LANES), lambda b, h: (b, h)),
        out_shape=jax.ShapeDtypeStruct((batch * seq, heads * LANES), _BF16),
        scratch_shapes=(
            [pltpu.VMEM((ATT_T, ATT_T), _F32)] * ATT_SCORE_BUFS
            + [pltpu.VMEM((ATT_T, LANES), _F32), pltpu.VMEM((ATT_T, 2 * LANES), _F32)] * 2),
        compiler_params=pltpu.CompilerParams(
            dimension_semantics=("arbitrary", "arbitrary"),
            vmem_limit_bytes=ATTENTION_VMEM),
        name=name,
    )(q, k, kx, v)


def _out_proj_kernel(om_ref, of_ref, gm_ref, gf_ref, w_ref, x_ref, o_ref):
    om = om_ref[...].astype(_F32)
    of = of_ref[...].astype(_F32)
    ym = jnp.dot((om * gm_ref[...]).astype(_BF16), w_ref[0:HEAD_WIDTH, :], preferred_element_type=_F32)
    yf = jnp.dot((of * gf_ref[...]).astype(_BF16), w_ref[HEAD_WIDTH:2 * HEAD_WIDTH, :],
                 preferred_element_type=_F32)
    o_ref[...] = x_ref[...] + _scale_rows(ym, _inv_rms(om)) + _scale_rows(yf, _inv_rms(of))


def _out_proj(o_mla, o_fox, gm, gf, w_out, x2d):
    T = x2d.shape[0]
    tm = OUT_TM
    return pl.pallas_call(
        _out_proj_kernel,
        grid=(T // tm,),
        in_specs=[
            pl.BlockSpec((tm, HEAD_WIDTH), lambda i: (i, 0)),
            pl.BlockSpec((tm, HEAD_WIDTH), lambda i: (i, 0)),
            pl.BlockSpec((1, HEAD_WIDTH), lambda i: (0, 0)),
            pl.BlockSpec((1, HEAD_WIDTH), lambda i: (0, 0)),
            pl.BlockSpec((2 * HEAD_WIDTH, D_MODEL), lambda i: (0, 0)),
            pl.BlockSpec((tm, D_MODEL), lambda i: (i, 0)),
        ],
        out_specs=pl.BlockSpec((tm, D_MODEL), lambda i: (i, 0)),
        out_shape=jax.ShapeDtypeStruct((T, D_MODEL), _F32),
        compiler_params=pltpu.CompilerParams(
            dimension_semantics=("arbitrary",),
            vmem_limit_bytes=OUT_PROJ_VMEM),
        name="out_proj",
    )(o_mla, o_fox, gm, gf, w_out, x2d)


def _ffn_kernel(x_ref, g_ref, wg_ref, wu_ref, wd_ref, gfin_ref, o_ref, h_scr, inv_scr):
    j = pl.program_id(1)

    def hidden_tile(h, inv):
        parts = []
        half = h.shape[0] // 2
        for r in range(2):
            rows = slice(r * half, (r + 1) * half)
            gate = _scale_rows(jnp.dot(h[rows], wg_ref[...], preferred_element_type=_F32), inv[rows])
            up = _scale_rows(jnp.dot(h[rows], wu_ref[...], preferred_element_type=_F32), inv[rows])
            a = (gate * (1.0 / (1.0 + jnp.exp(-gate))) * up).astype(_BF16)
            parts.append(jnp.dot(a, wd_ref[...], preferred_element_type=_F32))
        return jnp.concatenate(parts, axis=0)

    @pl.when(j == 0)
    def _():
        x = x_ref[...]
        h = (x * g_ref[...]).astype(_BF16)
        h_scr[...] = h
        inv = _inv_rms(x)
        inv_scr[...] = inv
        o_ref[...] = hidden_tile(h, inv)

    last = pl.num_programs(1) - 1

    @pl.when((j > 0) & (j < last))
    def _():
        o_ref[...] += hidden_tile(h_scr[...], inv_scr[...])

    @pl.when(j == last)
    def _():
        y = x_ref[...] + o_ref[...] + hidden_tile(h_scr[...], inv_scr[...])
        o_ref[...] = _rms_scale(y) * gfin_ref[...]


def _ffn(x1, g, w_gate, w_up, w_down, g_final):
    T = x1.shape[0]
    tm, th = FFN_TM, FFN_TH
    nh = FFN_HIDDEN // th

    def hid(i, j):
        return jnp.where(i % 2 == 0, j, nh - 1 - j)

    return pl.pallas_call(
        _ffn_kernel,
        grid=(T // tm, nh),
        in_specs=[
            pl.BlockSpec((tm, D_MODEL), lambda i, j: (i, 0)),
            pl.BlockSpec((1, D_MODEL), lambda i, j: (0, 0)),
            pl.BlockSpec((D_MODEL, th), lambda i, j: (0, hid(i, j))),
            pl.BlockSpec((D_MODEL, th), lambda i, j: (0, hid(i, j))),
            pl.BlockSpec((th, D_MODEL), lambda i, j: (hid(i, j), 0)),
            pl.BlockSpec((1, D_MODEL), lambda i, j: (0, 0)),
        ],
        out_specs=pl.BlockSpec((tm, D_MODEL), lambda i, j: (i, 0)),
        out_shape=jax.ShapeDtypeStruct((T, D_MODEL), _F32),
        scratch_shapes=[pltpu.VMEM((tm, D_MODEL), _BF16), pltpu.VMEM((tm, LANES), _F32)],
        compiler_params=pltpu.CompilerParams(
            dimension_semantics=("arbitrary", "arbitrary"),
            vmem_limit_bytes=FFN_VMEM),
        name="ffn",
    )(x1, g, w_gate, w_up, w_down, g_final)


def _prep_w_in(w):
    w = w.astype(_BF16)
    lat = Q_LORA + KV_LORA
    kr = w[:, lat:lat + QK_ROPE]
    fox0 = lat + QK_ROPE
    f = w[:, fox0 + 3 * HEAD_WIDTH:]
    pad = jnp.zeros((w.shape[0], LANES - 3 * FOX_HEADS), w.dtype)
    misc = jnp.concatenate([f, f, f, pad, w[:, :lat], kr, kr], axis=1)
    return misc, w[:, fox0:fox0 + 3 * HEAD_WIDTH]


def _prep_w_uq(w):
    w3 = w.reshape(Q_LORA, MLA_HEADS, QK_NOPE + QK_ROPE)
    w3 = jnp.concatenate([w3, w3[:, :, QK_NOPE:]], axis=2)
    return w3.reshape(Q_LORA, MLA_HEADS * QK_WIDTH).astype(_BF16)


def kernel(x, positions, g_attn_norm, w_in, b_forget, g_q_lat, w_uq, g_kv_lat, w_ukv, g_out_mla, g_out_fox, w_out, g_ffn_norm, w_gate, w_up, w_down, g_final_norm):
    B, S, D = x.shape
    T = B * S
    assert w_in.shape[0] == 1, "one layer (DEPTH == 1) is supported"
    inv_freq = ROPE_THETA ** (-jnp.arange(0, QK_ROPE, 2, dtype=_F32) / QK_ROPE)
    invf_row = jnp.tile(inv_freq, LANES // (QK_ROPE // 2)).reshape(1, LANES)
    pos_f = positions.astype(_F32).reshape(T, 1)
    x2d = x.reshape(T, D)

    b_row = jnp.concatenate(
        [b_forget[0]] * 3 + [jnp.zeros((LANES - 3 * FOX_HEADS,), _F32)]).reshape(1, LANES)
    w_misc, w_fox = _prep_w_in(w_in[0])
    misc, dec, rot, fox = _in_proj(x2d, g_attn_norm[0].reshape(1, D), w_misc, w_fox, b_row,
                                   pos_f, invf_row, S)
    mq, mkn, mkpe, mv = _mla_up(misc, rot, g_q_lat[0].reshape(1, Q_LORA),
                                g_kv_lat[0].reshape(1, KV_LORA), _prep_w_uq(w_uq[0]), w_ukv[0].astype(_BF16))

    def head_spec(width):
        return pl.BlockSpec((ATT_HEADS, S, width), lambda b, h: (h, b, 0))

    def slab_spec(slab):
        return pl.BlockSpec((None, ATT_HEADS, S, LANES), lambda b, h: (slab, h, b, 0))

    o_mla = _attention(mq, head_spec(QK_WIDTH), mkn, head_spec(LANES), mkpe, mv, head_spec(LANES),
                       B, S, MLA_HEADS, CHUNK, False, "mla_attn")
    o_fox = _attention(fox, slab_spec(0), fox, slab_spec(1), dec, fox, slab_spec(2),
                       B, S, FOX_HEADS, 1, True, "fox_attn")
    x1 = _out_proj(o_mla, o_fox, g_out_mla[0].reshape(1, HEAD_WIDTH),
                   g_out_fox[0].reshape(1, HEAD_WIDTH), w_out[0].astype(_BF16), x2d)
    out = _ffn(x1, g_ffn_norm[0].reshape(1, D), w_gate[0].astype(_BF16), w_up[0].astype(_BF16),
               w_down[0].astype(_BF16), g_final_norm.reshape(1, D))
    return out.reshape(B, S, D)
```
